```python
import math
import jax, jax.numpy as jnp
from jax import lax
import numpy as np

D_MODEL = 1024
BATCH = 8
SEQ = 4096
DEPTH = 2

CHUNK = 64
EPS = 1e-6
RET_HEADS = 4
RET_QK_DIM = 256
RET_V_DIM = 512
RET_Q_COLS = RET_HEADS * RET_QK_DIM
RET_V_COLS = RET_HEADS * RET_V_DIM
RET_IN_COLS = 2 * RET_Q_COLS + 2 * RET_V_COLS
ROPE_BASE = 10000.0
ATT_HEADS = 16
ATT_HEAD_DIM = D_MODEL // ATT_HEADS
PAST_CHUNKS = 8
BAND = (PAST_CHUNKS + 1) * CHUNK
REL_CLIP = 256
REL_TABLE = 2 * REL_CLIP + 1
FFN_HIDDEN = int(math.ceil(math.ceil(8 * D_MODEL / 3) / 256) * 256)

kernel_name = "yoco_retention_chunked_relbias_attention"


def rmsnorm(x, g):
    x32 = x.astype(jnp.float32)
    y = x32 * lax.rsqrt(jnp.mean(x32 * x32, axis=-1, keepdims=True) + EPS)
    return (y * g.astype(jnp.float32)).astype(x.dtype)


def swiglu_ffn(h, w_gu, w_down):
    gate, up = jnp.split(h @ w_gu, 2, axis=-1)
    return (jax.nn.silu(gate) * up) @ w_down


def rope(x, positions):
    half = x.shape[-1] // 2
    inv_freq = ROPE_BASE ** (-jnp.arange(half, dtype=jnp.float32) / half)
    ang = positions.astype(jnp.float32)[:, None] * inv_freq[None, :]
    cos = jnp.cos(ang)[None, :, None, :]
    sin = jnp.sin(ang)[None, :, None, :]
    x1, x2 = x[..., :half], x[..., half:]
    return jnp.concatenate([x1 * cos - x2 * sin, x1 * sin + x2 * cos], axis=-1).astype(x.dtype)


def retention_mixer(h, w_in, gn_g, w_o):
    b, s, _ = h.shape
    n = s // CHUNK
    proj = h @ w_in
    q, k, v, g = jnp.split(proj, [RET_Q_COLS, 2 * RET_Q_COLS, 2 * RET_Q_COLS + RET_V_COLS], axis=-1)
    pos = jnp.arange(s)
    q = rope(q.reshape(b, s, RET_HEADS, RET_QK_DIM), pos)
    k = rope(k.reshape(b, s, RET_HEADS, RET_QK_DIM), pos) * (RET_QK_DIM ** -0.5)
    v = v.reshape(b, s, RET_HEADS, RET_V_DIM)
    to_chunks = lambda t: t.reshape(b, n, CHUNK, RET_HEADS, t.shape[-1]).transpose(1, 0, 3, 2, 4)
    qc, kc, vc = to_chunks(q), to_chunks(k), to_chunks(v)

    log_gamma = jnp.log(1.0 - 2.0 ** (-5.0 - jnp.arange(RET_HEADS, dtype=jnp.float32)))
    lg = log_gamma[:, None]
    t = jnp.arange(CHUNK, dtype=jnp.float32)
    intra = jnp.exp(lg[:, :, None] * jnp.abs(t[:, None] - t[None, :]))
    q_dec = jnp.exp(lg * (t + 1.0))[:, :, None]
    k_dec = jnp.exp(lg * (CHUNK - 1.0 - t))[:, :, None]
    s_dec = jnp.exp(lg * CHUNK)[:, :, None]

    def step(state, inp):
        qi, ki, vi = inp
        scores = jnp.einsum('bhtd,bhsd->bhts', qi, ki) * intra
        o = jnp.einsum('bhts,bhsv->bhtv', scores, vi) + jnp.einsum('bhtd,bhdv->bhtv', qi * q_dec, state)
        state = state * s_dec + jnp.einsum('bhsd,bhsv->bhdv', ki * k_dec, vi)
        return state, o

    state0 = jnp.zeros((b, RET_HEADS, RET_QK_DIM, RET_V_DIM), dtype=jnp.result_type(qc.dtype, intra.dtype))
    _, o = lax.scan(step, state0, (qc, kc, vc))
    o = o.transpose(1, 0, 3, 2, 4).reshape(b, s, RET_HEADS, RET_V_DIM)
    o = rmsnorm(o, gn_g.reshape(RET_HEADS, RET_V_DIM)).reshape(b, s, RET_V_COLS).astype(h.dtype)
    return (jax.nn.silu(g) * o) @ w_o


def shared_kv(h, kv_norm_g, w_kv, k_norm_g):
    b, s, _ = h.shape
    u = rmsnorm(h, kv_norm_g)
    k, v = jnp.split(u @ w_kv, 2, axis=-1)
    k = rmsnorm(k.reshape(b, s, ATT_HEADS, ATT_HEAD_DIM), k_norm_g)
    v = v.reshape(b, s, ATT_HEADS, ATT_HEAD_DIM)
    pad = ((0, 0), (PAST_CHUNKS * CHUNK, 0), (0, 0), (0, 0))
    return jnp.pad(k, pad), jnp.pad(v, pad)


def chunk_band_attention(h, w_q, q_norm_g, rel_bias, w_o, k_pad, v_pad):
    b, s, _ = h.shape
    n = s // CHUNK
    q = rmsnorm((h @ w_q).reshape(b, s, ATT_HEADS, ATT_HEAD_DIM), q_norm_g) * (ATT_HEAD_DIM ** -0.5)
    qc = q.reshape(b, n, CHUNK, ATT_HEADS, ATT_HEAD_DIM).swapaxes(0, 1)
    t = jnp.arange(CHUNK)
    j = jnp.arange(BAND)
    dist = PAST_CHUNKS * CHUNK + t[:, None] - j[None, :]
    bias = rel_bias.astype(jnp.float32)[:, jnp.clip(dist, -REL_CLIP, REL_CLIP) + REL_CLIP]

    def one_chunk(args):
        i, qi = args
        kb = lax.dynamic_slice_in_dim(k_pad, i * CHUNK, BAND, axis=1)
        vb = lax.dynamic_slice_in_dim(v_pad, i * CHUNK, BAND, axis=1)
        sc = jnp.einsum('bthd,bshd->bhts', qi, kb).astype(jnp.float32) + bias
        valid = j >= (PAST_CHUNKS - i) * CHUNK
        sc = jnp.where(valid[None, None, None, :], sc, -jnp.inf)
        p = jax.nn.softmax(sc, axis=-1).astype(vb.dtype)
        return jnp.einsum('bhts,bshd->bthd', p, vb)

    o = lax.map(one_chunk, (jnp.arange(n), qc))
    o = o.swapaxes(0, 1).reshape(b, s, ATT_HEADS * ATT_HEAD_DIM)
    return o @ w_o


def setup_inputs(seed: int = 0) -> dict:
    key = jax.random.key(seed)
    ks = iter(jax.random.split(key, 32))
    n_a = DEPTH // 2
    n_b = DEPTH - n_a
    f32 = jnp.float32

    def w(shape, fan_in):
        return jax.random.normal(next(ks), shape, f32) * (fan_in ** -0.5)

    def gain(shape):
        return 1.0 + 0.1 * jax.random.normal(next(ks), shape, f32)

    return {
        "x": jax.random.normal(next(ks), (BATCH, SEQ, D_MODEL), f32),
        "a_norm_g": gain((n_a, D_MODEL)),
        "a_w_in": w((n_a, D_MODEL, RET_IN_COLS), D_MODEL),
        "a_gn_g": gain((n_a, RET_V_COLS)),
        "a_w_o": w((n_a, RET_V_COLS, D_MODEL), RET_V_COLS),
        "a_ffn_norm_g": gain((n_a, D_MODEL)),
        "a_w_gu": w((n_a, D_MODEL, 2 * FFN_HIDDEN), D_MODEL),
        "a_w_down": w((n_a, FFN_HIDDEN, D_MODEL), FFN_HIDDEN),
        "kv_norm_g": gain((D_MODEL,)),
        "w_kv": w((D_MODEL, 2 * D_MODEL), D_MODEL),
        "k_norm_g": gain((ATT_HEAD_DIM,)),
        "b_norm_g": gain((n_b, D_MODEL)),
        "b_w_q": w((n_b, D_MODEL, D_MODEL), D_MODEL),
        "b_q_norm_g": gain((n_b, ATT_HEAD_DIM)),
        "b_rel_bias": 0.5 * jax.random.normal(next(ks), (n_b, ATT_HEADS, REL_TABLE), f32),
        "b_w_o": w((n_b, D_MODEL, D_MODEL), D_MODEL),
        "b_ffn_norm_g": gain((n_b, D_MODEL)),
        "b_w_gu": w((n_b, D_MODEL, 2 * FFN_HIDDEN), D_MODEL),
        "b_w_down": w((n_b, FFN_HIDDEN, D_MODEL), FFN_HIDDEN),
    }


def reference(x, a_norm_g, a_w_in, a_gn_g, a_w_o, a_ffn_norm_g, a_w_gu, a_w_down,
              kv_norm_g, w_kv, k_norm_g,
              b_norm_g, b_w_q, b_q_norm_g, b_rel_bias, b_w_o, b_ffn_norm_g, b_w_gu, b_w_down):
    n_a = DEPTH // 2
    k_pad = v_pad = None
    for layer in range(DEPTH):
        if layer < n_a:
            i = layer
            x = x + retention_mixer(rmsnorm(x, a_norm_g[i]), a_w_in[i], a_gn_g[i], a_w_o[i])
            x = x + swiglu_ffn(rmsnorm(x, a_ffn_norm_g[i]), a_w_gu[i], a_w_down[i])
        else:
            if layer == n_a:
                k_pad, v_pad = shared_kv(x, kv_norm_g, w_kv, k_norm_g)
            i = layer - n_a
            x = x + chunk_band_attention(rmsnorm(x, b_norm_g[i]), b_w_q[i], b_q_norm_g[i],
                                         b_rel_bias[i], b_w_o[i], k_pad, v_pad)
            x = x + swiglu_ffn(rmsnorm(x, b_ffn_norm_g[i]), b_w_gu[i], b_w_down[i])
    return x
```

```python
import functools

import jax
import jax.numpy as jnp
from jax import lax
from jax.experimental import pallas as pl
from jax.experimental.pallas import tpu as pltpu

F32 = jnp.float32
BF16 = jnp.bfloat16

CHUNK = 64
EPS = 1e-6
RET_HEADS = 4
RET_QK_DIM = 256
RET_V_DIM = 512
ROPE_BASE = 10000.0
ATT_HEADS = 16
ATT_HEAD_DIM = 64
PAST_CHUNKS = 8
REL_CLIP = 256

ROW_TILE = 512
SEQ_BLOCK = 256
CHUNKS_PER_BLOCK = SEQ_BLOCK // CHUNK
WINDOW_BLOCKS = 1 + PAST_CHUNKS // CHUNKS_PER_BLOCK
WINDOW = WINDOW_BLOCKS * SEQ_BLOCK
BIAS_VEC = 1024
REL_PAD = 640
FFN_CHUNK = 256
VMEM_LIMIT_BYTES = 56 * 1024 * 1024


def _resident(shape):
    zeros = (0,) * len(shape)
    return pl.BlockSpec(shape, lambda *_: zeros, pipeline_mode=pl.Buffered(1))


def _params(*semantics):
    return pltpu.CompilerParams(dimension_semantics=semantics,
                                vmem_limit_bytes=VMEM_LIMIT_BYTES)


def _rms_scale(x):
    return x * lax.rsqrt(jnp.mean(x * x, axis=-1, keepdims=True) + EPS)


def _dot(a, b):
    return jnp.dot(a, b, preferred_element_type=F32)


def _dot_nt(a, b):
    return lax.dot_general(a, b, (((1,), (1,)), ((), ())), preferred_element_type=F32)


def _silu(x):
    return x * jax.nn.sigmoid(x)


def _inproj_kernel(x_ref, g_ref, wq_ref, wkT_ref, wv_ref, wg_ref,
                   cos_ref, sin_ref, cosT_ref, sinT_ref,
                   q_ref, kT_ref, v_ref, gate_ref):
    h = (_rms_scale(x_ref[...]) * g_ref[...]).astype(BF16)
    cos, sin = cos_ref[...], sin_ref[...]
    cosT, sinT = cosT_ref[...], sinT_ref[...]
    half = RET_QK_DIM // 2
    k_scale = RET_QK_DIM ** -0.5
    for hd in range(RET_HEADS):
        lo = hd * RET_QK_DIM
        qh = _dot(h, wq_ref[:, lo:lo + RET_QK_DIM])
        x1, x2 = qh[:, :half], qh[:, half:]
        q_ref[:, lo:lo + half] = (x1 * cos - x2 * sin).astype(BF16)
        q_ref[:, lo + half:lo + RET_QK_DIM] = (x1 * sin + x2 * cos).astype(BF16)
        kh = _dot_nt(wkT_ref[lo:lo + RET_QK_DIM, :], h)
        y1, y2 = kh[:half], kh[half:]
        kT_ref[0, lo:lo + half, :] = ((y1 * cosT - y2 * sinT) * k_scale).astype(BF16)
        kT_ref[0, lo + half:lo + RET_QK_DIM, :] = ((y1 * sinT + y2 * cosT) * k_scale).astype(BF16)
    for hd in range(RET_HEADS):
        lo = hd * RET_V_DIM
        v_ref[:, lo:lo + RET_V_DIM] = _dot(h, wv_ref[:, lo:lo + RET_V_DIM]).astype(BF16)
        gate_ref[:, lo:lo + RET_V_DIM] = _silu(_dot(h, wg_ref[:, lo:lo + RET_V_DIM])).astype(BF16)


def _inproj(xf, norm_g, w_in, batch, seq):
    t, d = xf.shape
    q_cols = RET_HEADS * RET_QK_DIM
    v_cols = RET_HEADS * RET_V_DIM
    half = RET_QK_DIM // 2
    wq = w_in[:, :q_cols].astype(BF16)
    wkT = w_in[:, q_cols:2 * q_cols].T.astype(BF16)
    wv = w_in[:, 2 * q_cols:2 * q_cols + v_cols].astype(BF16)
    wg = w_in[:, 2 * q_cols + v_cols:].astype(BF16)
    inv_freq = ROPE_BASE ** (-jnp.arange(half, dtype=F32) / half)
    ang = jnp.arange(seq).astype(F32)[:, None] * inv_freq[None, :]
    cos, sin = jnp.cos(ang), jnp.sin(ang)
    tiles_per_seq = seq // ROW_TILE
    row_spec = lambda cols: pl.BlockSpec((ROW_TILE, cols), lambda i: (i, 0))
    return pl.pallas_call(
        _inproj_kernel,
        grid=(t // ROW_TILE,),
        in_specs=[
            row_spec(d),
            _resident((1, d)),
            _resident((d, q_cols)),
            _resident((q_cols, d)),
            _resident((d, v_cols)),
            _resident((d, v_cols)),
            pl.BlockSpec((ROW_TILE, half), lambda i: (i % tiles_per_seq, 0)),
            pl.BlockSpec((ROW_TILE, half), lambda i: (i % tiles_per_seq, 0)),
            pl.BlockSpec((half, ROW_TILE), lambda i: (0, i % tiles_per_seq)),
            pl.BlockSpec((half, ROW_TILE), lambda i: (0, i % tiles_per_seq)),
        ],
        out_specs=[
            row_spec(q_cols),
            pl.BlockSpec((1, q_cols, ROW_TILE),
                         lambda i: (i // tiles_per_seq, 0, i % tiles_per_seq)),
            row_spec(v_cols),
            row_spec(v_cols),
        ],
        out_shape=[
            jax.ShapeDtypeStruct((t, q_cols), BF16),
            jax.ShapeDtypeStruct((batch, q_cols, seq), BF16),
            jax.ShapeDtypeStruct((t, v_cols), BF16),
            jax.ShapeDtypeStruct((t, v_cols), BF16),
        ],
        compiler_params=_params("parallel"),
        name="inproj",
    )(xf, norm_g.reshape(1, d), wq, wkT, wv, wg, cos, sin, cos.T, sin.T)


def _retention_kernel(q_ref, kT_ref, v_ref, gate_ref, x_ref,
                      qdec_ref, kdec_ref, intra_ref, sdec_ref, gn_ref, wo_ref,
                      o_ref, state_ref, y_ref):
    @pl.when(pl.program_id(1) == 0)
    def _():
        state_ref[...] = jnp.zeros_like(state_ref)

    for hd in range(RET_HEADS):
        ql, vl = hd * RET_QK_DIM, hd * RET_V_DIM
        qh = q_ref[:, ql:ql + RET_QK_DIM]
        kTh = kT_ref[0, ql:ql + RET_QK_DIM, :]
        vh = v_ref[:, vl:vl + RET_V_DIM]
        state = state_ref[hd]
        scores = _dot(qh, kTh) * intra_ref[hd]
        qd = (qh.astype(F32) * qdec_ref[:, ql:ql + RET_QK_DIM]).astype(BF16)
        o = _dot(scores.astype(BF16), vh) + _dot(qd, state.astype(BF16))
        kd = (kTh.astype(F32) * kdec_ref[ql:ql + RET_QK_DIM, :]).astype(BF16)
        state_ref[hd] = state * sdec_ref[hd] + _dot(kd, vh)
        on = _rms_scale(o) * gn_ref[:, vl:vl + RET_V_DIM]
        y_ref[:, vl:vl + RET_V_DIM] = (gate_ref[:, vl:vl + RET_V_DIM].astype(F32) * on).astype(BF16)
    o_ref[...] = x_ref[...] + _dot(y_ref[...], wo_ref[...])


def _retention_tables():
    lg = jnp.log(1.0 - 2.0 ** (-5.0 - jnp.arange(RET_HEADS, dtype=F32)))
    pos = jnp.arange(SEQ_BLOCK, dtype=F32)
    chunk_of = jnp.arange(SEQ_BLOCK) // CHUNK
    visible = chunk_of[None, :] <= chunk_of[:, None]
    intra = jnp.exp(lg[:, None, None] * jnp.abs(pos[:, None] - pos[None, :]))
    intra = jnp.where(visible[None], intra, 0.0)
    q_dec = jnp.exp(lg[:, None] * (pos[None, :] + 1.0))
    k_dec = jnp.exp(lg[:, None] * (SEQ_BLOCK - 1.0 - pos[None, :]))
    s_dec = jnp.exp(lg * SEQ_BLOCK)
    qdec_full = jnp.repeat(q_dec.T, RET_QK_DIM, axis=1)
    kdec_full = jnp.repeat(k_dec, RET_QK_DIM, axis=0)
    sdec_full = jnp.broadcast_to(s_dec[:, None, None], (RET_HEADS, 1, RET_V_DIM))
    return qdec_full, kdec_full, intra, sdec_full


def _retention(q, kT, v, gate, xf, gn_g, w_o, batch, seq):
    t, d = xf.shape
    q_cols = RET_HEADS * RET_QK_DIM
    v_cols = RET_HEADS * RET_V_DIM
    nblk = seq // SEQ_BLOCK
    qdec, kdec, intra, sdec = _retention_tables()
    row_spec = lambda cols: pl.BlockSpec((SEQ_BLOCK, cols), lambda b, j: (b * nblk + j, 0))
    return pl.pallas_call(
        _retention_kernel,
        grid=(batch, nblk),
        in_specs=[
            row_spec(q_cols),
            pl.BlockSpec((1, q_cols, SEQ_BLOCK), lambda b, j: (b, 0, j)),
            row_spec(v_cols),
            row_spec(v_cols),
            row_spec(d),
            _resident((SEQ_BLOCK, q_cols)),
            _resident((q_cols, SEQ_BLOCK)),
            _resident((RET_HEADS, SEQ_BLOCK, SEQ_BLOCK)),
            _resident((RET_HEADS, 1, RET_V_DIM)),
            _resident((1, v_cols)),
            _resident((v_cols, d)),
        ],
        out_specs=row_spec(d),
        out_shape=jax.ShapeDtypeStruct((t, d), F32),
        scratch_shapes=[
            pltpu.VMEM((RET_HEADS, RET_QK_DIM, RET_V_DIM), F32),
            pltpu.VMEM((SEQ_BLOCK, v_cols), BF16),
        ],
        compiler_params=_params("parallel", "arbitrary"),
        name="retention",
    )(q, kT, v, gate, xf, qdec, kdec, intra, sdec, gn_g.reshape(1, v_cols), w_o.astype(BF16))


def _ffn_kernel(x_ref, g_ref, wgu_ref, wd_ref, o_ref, act_ref):
    x = x_ref[...]
    h = (_rms_scale(x) * g_ref[...]).astype(BF16)
    hidden = wd_ref.shape[0]
    for c in range(0, hidden, FFN_CHUNK):
        gate = _dot(h, wgu_ref[:, c:c + FFN_CHUNK])
        up = _dot(h, wgu_ref[:, hidden + c:hidden + c + FFN_CHUNK])
        act_ref[:, c:c + FFN_CHUNK] = (_silu(gate) * up).astype(BF16)
    o_ref[...] = x + _dot(act_ref[...], wd_ref[...])


def _ffn(xf, norm_g, w_gu, w_down):
    t, d = xf.shape
    hidden = w_down.shape[0]
    row_spec = pl.BlockSpec((ROW_TILE, d), lambda i: (i, 0))
    return pl.pallas_call(
        _ffn_kernel,
        grid=(t // ROW_TILE,),
        in_specs=[row_spec, _resident((1, d)), _resident((d, 2 * hidden)), _resident((hidden, d))],
        out_specs=row_spec,
        out_shape=jax.ShapeDtypeStruct((t, d), F32),
        scratch_shapes=[pltpu.VMEM((ROW_TILE, hidden), BF16)],
        compiler_params=_params("parallel"),
        name="ffn",
    )(xf, norm_g.reshape(1, d), w_gu.astype(BF16), w_down.astype(BF16))


def _kvq_kernel(x_ref, gkv_ref, gq_ref, wkT_ref, wv_ref, wq_ref, kg_ref,
                q_ref, kT_ref, v_ref):
    xn = _rms_scale(x_ref[...])
    u = (xn * gkv_ref[...]).astype(BF16)
    hq = (xn * gq_ref[...]).astype(BF16)
    d = x_ref.shape[1]
    for c in range(0, d, 512):
        v_ref[:, c:c + 512] = _dot(u, wv_ref[:, c:c + 512]).astype(BF16)
        q_ref[:, c:c + 512] = _dot(hq, wq_ref[:, c:c + 512]).astype(BF16)
    kg = kg_ref[...]
    for c in range(0, d, 256):
        kT = _dot_nt(wkT_ref[c:c + 256, :], u)
        for j in range(0, 256, ATT_HEAD_DIM):
            kh = kT[j:j + ATT_HEAD_DIM]
            inv = lax.rsqrt(jnp.mean(kh * kh, axis=0, keepdims=True) + EPS)
            kT_ref[0, c + j:c + j + ATT_HEAD_DIM, :] = (kh * inv * kg).astype(BF16)


def _kvq(xf, kv_norm_g, w_kv, k_norm_g, q_in_norm_g, w_q, batch, seq):
    t, d = xf.shape
    tiles_per_seq = seq // ROW_TILE
    row_spec = pl.BlockSpec((ROW_TILE, d), lambda i: (i, 0))
    return pl.pallas_call(
        _kvq_kernel,
        grid=(t // ROW_TILE,),
        in_specs=[
            row_spec, _resident((1, d)), _resident((1, d)),
            _resident((d, d)), _resident((d, d)), _resident((d, d)),
            _resident((ATT_HEAD_DIM, 1)),
        ],
        out_specs=[
            row_spec,
            pl.BlockSpec((1, d, ROW_TILE), lambda i: (i // tiles_per_seq, 0, i % tiles_per_seq)),
            row_spec,
        ],
        out_shape=[
            jax.ShapeDtypeStruct((t, d), BF16),
            jax.ShapeDtypeStruct((batch, d, seq), BF16),
            jax.ShapeDtypeStruct((t, d), BF16),
        ],
        compiler_params=_params("parallel"),
        name="kvq",
    )(xf, kv_norm_g.reshape(1, d), q_in_norm_g.reshape(1, d),
      w_kv[:, :d].T.astype(BF16), w_kv[:, d:].astype(BF16), w_q.astype(BF16),
      k_norm_g.reshape(ATT_HEAD_DIM, 1))


def _bias_table_kernel(rb_ref, tab_ref):
    lane = lax.broadcasted_iota(jnp.int32, (REL_PAD, BIAS_VEC), 1)
    entry = lax.broadcasted_iota(jnp.int32, (REL_PAD, BIAS_VEC), 0)
    key_minus_query = jnp.where(lane < WINDOW, lane, lane - BIAS_VEC)
    dist = (WINDOW - SEQ_BLOCK) - key_minus_query
    idx = jnp.clip(dist, -REL_CLIP, REL_CLIP) + REL_CLIP
    onehot = (entry == idx).astype(BF16)
    rb = rb_ref[...]
    hi = rb.astype(BF16)
    rest = rb - hi.astype(F32)
    mid = rest.astype(BF16)
    low = (rest - mid.astype(F32)).astype(BF16)
    vec = _dot(hi, onehot) + _dot(mid, onehot) + _dot(low, onehot)
    q_chunk = lax.broadcasted_iota(jnp.int32, (SEQ_BLOCK, WINDOW), 0) // CHUNK
    k_chunk = lax.broadcasted_iota(jnp.int32, (SEQ_BLOCK, WINDOW), 1) // CHUNK
    visible = (k_chunk >= q_chunk) & (k_chunk <= q_chunk + PAST_CHUNKS)
    for h in range(ATT_HEADS):
        rows = jnp.broadcast_to(vec[h:h + 1, :], (SEQ_BLOCK, BIAS_VEC))
        toeplitz = pltpu.roll(rows, 0, 1, stride=1, stride_axis=0)
        tab_ref[h] = jnp.where(visible, toeplitz[:, :WINDOW], -jnp.inf)


def _bias_table(rel_bias):
    heads, table = rel_bias.shape
    rb = jnp.pad(rel_bias, ((0, 0), (0, REL_PAD - table)))
    return pl.pallas_call(
        _bias_table_kernel,
        out_shape=jax.ShapeDtypeStruct((heads, SEQ_BLOCK, WINDOW), F32),
        compiler_params=pltpu.CompilerParams(vmem_limit_bytes=VMEM_LIMIT_BYTES),
        name="bias_table",
    )(rb)


def _attention_kernel(q_ref, k0_ref, k1_ref, k2_ref, v0_ref, v1_ref, v2_ref, x_ref,
                      gq_ref, tab_ref, wo_ref, o_ref, ctx_ref):
    j = pl.program_id(1)
    k_refs = (k0_ref, k1_ref, k2_ref)
    v_refs = (v0_ref, v1_ref, v2_ref)
    block_mask = [jnp.where(j + p >= WINDOW_BLOCKS - 1, 0.0, -jnp.inf).astype(F32)
                  for p in range(WINDOW_BLOCKS)]
    q = q_ref[...].astype(F32)
    qs = (q * gq_ref[...]).astype(BF16)
    for h in range(ATT_HEADS):
        lo = h * ATT_HEAD_DIM
        qh = q[:, lo:lo + ATT_HEAD_DIM]
        inv = lax.rsqrt(jnp.mean(qh * qh, axis=-1, keepdims=True) + EPS)
        qsh = qs[:, lo:lo + ATT_HEAD_DIM]
        s = [_dot(qsh, k_refs[p][0, lo:lo + ATT_HEAD_DIM, :]) * inv
             + tab_ref[h, :, p * SEQ_BLOCK:(p + 1) * SEQ_BLOCK] + block_mask[p]
             for p in range(WINDOW_BLOCKS)]
        m = functools.reduce(jnp.maximum, [jnp.max(sp, axis=-1, keepdims=True) for sp in s])
        e = [jnp.exp(sp - m) for sp in s]
        denom = functools.reduce(jnp.add, [jnp.sum(ep, axis=-1, keepdims=True) for ep in e])
        ctx = functools.reduce(jnp.add, [
            _dot(e[p].astype(BF16), v_refs[p][:, lo:lo + ATT_HEAD_DIM])
            for p in range(WINDOW_BLOCKS)])
        ctx_ref[:, lo:lo + ATT_HEAD_DIM] = (ctx / denom).astype(BF16)
    o_ref[...] = x_ref[...] + _dot(ctx_ref[...], wo_ref[...])


def _attention(q, kT, v, xf, q_norm_g, table, w_o, batch, seq):
    t, d = xf.shape
    nblk = seq // SEQ_BLOCK
    back = WINDOW_BLOCKS - 1
    gq = jnp.tile(q_norm_g * (ATT_HEAD_DIM ** -0.5), ATT_HEADS).reshape(1, d)
    row_spec = pl.BlockSpec((SEQ_BLOCK, d), lambda b, j: (b * nblk + j, 0))

    def k_spec(p):
        return pl.BlockSpec((1, d, SEQ_BLOCK),
                            lambda b, j: (b, 0, jnp.maximum(j - back + p, 0)))

    def v_spec(p):
        return pl.BlockSpec((SEQ_BLOCK, d),
                            lambda b, j: (b * nblk + jnp.maximum(j - back + p, 0), 0))

    return pl.pallas_call(
        _attention_kernel,
        grid=(batch, nblk),
        in_specs=[row_spec]
                 + [k_spec(p) for p in range(WINDOW_BLOCKS)]
                 + [v_spec(p) for p in range(WINDOW_BLOCKS)]
                 + [row_spec, _resident((1, d)),
                    _resident((ATT_HEADS, SEQ_BLOCK, WINDOW)), _resident((d, d))],
        out_specs=row_spec,
        out_shape=jax.ShapeDtypeStruct((t, d), F32),
        scratch_shapes=[pltpu.VMEM((SEQ_BLOCK, d), BF16)],
        compiler_params=_params("parallel", "arbitrary"),
        name="attention",
    )(q, kT, kT, kT, v, v, v, xf, gq, table, w_o.astype(BF16))


def kernel(x, a_norm_g, a_w_in, a_gn_g, a_w_o, a_ffn_norm_g, a_w_gu, a_w_down,
           kv_norm_g, w_kv, k_norm_g,
           b_norm_g, b_w_q, b_q_norm_g, b_rel_bias, b_w_o, b_ffn_norm_g, b_w_gu, b_w_down):
    batch, seq, d = x.shape
    assert seq % ROW_TILE == 0 and seq % SEQ_BLOCK == 0
    assert WINDOW_BLOCKS * CHUNKS_PER_BLOCK >= PAST_CHUNKS + CHUNKS_PER_BLOCK
    assert b_norm_g.shape[0] == 1, "the query projection is fused with the shared K/V projection"
    xf = x.reshape(batch * seq, d)
    for i in range(a_norm_g.shape[0]):
        q, kT, v, gate = _inproj(xf, a_norm_g[i], a_w_in[i], batch, seq)
        xf = _retention(q, kT, v, gate, xf, a_gn_g[i], a_w_o[i], batch, seq)
        xf = _ffn(xf, a_ffn_norm_g[i], a_w_gu[i], a_w_down[i])
    q, kT, v = _kvq(xf, kv_norm_g, w_kv, k_norm_g, b_norm_g[0], b_w_q[0], batch, seq)
    table = _bias_table(b_rel_bias[0])
    xf = _attention(q, kT, v, xf, b_q_norm_g[0], table, b_w_o[0], batch, seq)
    xf = _ffn(xf, b_ffn_norm_g[0], b_w_gu[0], b_w_down[0])
    return xf.reshape(batch, seq, d)
```

```python
import functools

import jax
import jax.numpy as jnp
from jax import lax
from jax.experimental import pallas as pl
from jax.experimental.pallas import tpu as pltpu

F32 = jnp.float32
BF16 = jnp.bfloat16

CHUNK = 64
EPS = 1e-6
RET_HEADS = 4
RET_QK_DIM = 256
RET_V_DIM = 512
ROPE_BASE = 10000.0
ATT_HEADS = 16
ATT_HEAD_DIM = 64
PAST_CHUNKS = 8
REL_CLIP = 256

ROW_TILE = 512
SEQ_BLOCK = 256
CHUNKS_PER_BLOCK = SEQ_BLOCK // CHUNK
WINDOW_BLOCKS = 1 + PAST_CHUNKS // CHUNKS_PER_BLOCK
LANE = 128
BAND = (PAST_CHUNKS + 1) * CHUNK
PAIR_KEYS = BAND + CHUNK
BIAS_VEC = 1024
REL_PAD = 640
FFN_CHUNK = 256
VMEM_LIMIT_BYTES = 56 * 1024 * 1024


def _resident(shape):
    zeros = (0,) * len(shape)
    return pl.BlockSpec(shape, lambda *_: zeros, pipeline_mode=pl.Buffered(1))


def _params(*semantics):
    return pltpu.CompilerParams(dimension_semantics=semantics,
                                vmem_limit_bytes=VMEM_LIMIT_BYTES)


def _rms_scale(x):
    return x * lax.rsqrt(jnp.mean(x * x, axis=-1, keepdims=True) + EPS)


def _dot(a, b):
    return jnp.dot(a, b, preferred_element_type=F32)


def _dot_nt(a, b):
    return lax.dot_general(a, b, (((1,), (1,)), ((), ())), preferred_element_type=F32)


def _silu(x):
    return x * jax.nn.sigmoid(x)


def _inproj_kernel(x_ref, g_ref, wq_ref, wkT_ref, wv_ref, wg_ref,
                   cos_ref, sin_ref, cosT_ref, sinT_ref,
                   q_ref, kT_ref, v_ref, gate_ref):
    h = (_rms_scale(x_ref[...]) * g_ref[...]).astype(BF16)
    cos, sin = cos_ref[...], sin_ref[...]
    cosT, sinT = cosT_ref[...], sinT_ref[...]
    half = RET_QK_DIM // 2
    k_scale = RET_QK_DIM ** -0.5
    for hd in range(RET_HEADS):
        lo = hd * RET_QK_DIM
        qh = _dot(h, wq_ref[:, lo:lo + RET_QK_DIM])
        x1, x2 = qh[:, :half], qh[:, half:]
        q_ref[:, lo:lo + half] = (x1 * cos - x2 * sin).astype(BF16)
        q_ref[:, lo + half:lo + RET_QK_DIM] = (x1 * sin + x2 * cos).astype(BF16)
        kh = _dot_nt(wkT_ref[lo:lo + RET_QK_DIM, :], h)
        y1, y2 = kh[:half], kh[half:]
        kT_ref[0, lo:lo + half, :] = ((y1 * cosT - y2 * sinT) * k_scale).astype(BF16)
        kT_ref[0, lo + half:lo + RET_QK_DIM, :] = ((y1 * sinT + y2 * cosT) * k_scale).astype(BF16)
    for hd in range(RET_HEADS):
        lo = hd * RET_V_DIM
        v_ref[:, lo:lo + RET_V_DIM] = _dot(h, wv_ref[:, lo:lo + RET_V_DIM]).astype(BF16)
        gate_ref[:, lo:lo + RET_V_DIM] = _silu(_dot(h, wg_ref[:, lo:lo + RET_V_DIM])).astype(BF16)


def _inproj(xf, norm_g, w_in, batch, seq):
    t, d = xf.shape
    q_cols = RET_HEADS * RET_QK_DIM
    v_cols = RET_HEADS * RET_V_DIM
    half = RET_QK_DIM // 2
    wq = w_in[:, :q_cols].astype(BF16)
    wkT = w_in[:, q_cols:2 * q_cols].T.astype(BF16)
    wv = w_in[:, 2 * q_cols:2 * q_cols + v_cols].astype(BF16)
    wg = w_in[:, 2 * q_cols + v_cols:].astype(BF16)
    inv_freq = ROPE_BASE ** (-jnp.arange(half, dtype=F32) / half)
    ang = jnp.arange(seq).astype(F32)[:, None] * inv_freq[None, :]
    cos, sin = jnp.cos(ang), jnp.sin(ang)
    tiles_per_seq = seq // ROW_TILE
    row_spec = lambda cols: pl.BlockSpec((ROW_TILE, cols), lambda i: (i, 0))
    return pl.pallas_call(
        _inproj_kernel,
        grid=(t // ROW_TILE,),
        in_specs=[
            row_spec(d),
            _resident((1, d)),
            _resident((d, q_cols)),
            _resident((q_cols, d)),
            _resident((d, v_cols)),
            _resident((d, v_cols)),
            pl.BlockSpec((ROW_TILE, half), lambda i: (i % tiles_per_seq, 0)),
            pl.BlockSpec((ROW_TILE, half), lambda i: (i % tiles_per_seq, 0)),
            pl.BlockSpec((half, ROW_TILE), lambda i: (0, i % tiles_per_seq)),
            pl.BlockSpec((half, ROW_TILE), lambda i: (0, i % tiles_per_seq)),
        ],
        out_specs=[
            row_spec(q_cols),
            pl.BlockSpec((1, q_cols, ROW_TILE),
                         lambda i: (i // tiles_per_seq, 0, i % tiles_per_seq)),
            row_spec(v_cols),
            row_spec(v_cols),
        ],
        out_shape=[
            jax.ShapeDtypeStruct((t, q_cols), BF16),
            jax.ShapeDtypeStruct((batch, q_cols, seq), BF16),
            jax.ShapeDtypeStruct((t, v_cols), BF16),
            jax.ShapeDtypeStruct((t, v_cols), BF16),
        ],
        compiler_params=_params("parallel"),
        name="inproj",
    )(xf, norm_g.reshape(1, d), wq, wkT, wv, wg, cos, sin, cos.T, sin.T)


def _retention_kernel(q_ref, kT_ref, v_ref, gate_ref, x_ref,
                      qdec_ref, kdec_ref, intra_ref, sdec_ref, gn_ref, wo_ref,
                      o_ref, state_ref, y_ref):
    @pl.when(pl.program_id(1) == 0)
    def _():
        state_ref[...] = jnp.zeros_like(state_ref)

    for hd in range(RET_HEADS):
        ql, vl = hd * RET_QK_DIM, hd * RET_V_DIM
        qh = q_ref[:, ql:ql + RET_QK_DIM]
        kTh = kT_ref[0, ql:ql + RET_QK_DIM, :]
        vh = v_ref[:, vl:vl + RET_V_DIM]
        state = state_ref[hd]
        scores = _dot(qh, kTh) * intra_ref[hd]
        qd = (qh.astype(F32) * qdec_ref[:, ql:ql + RET_QK_DIM]).astype(BF16)
        o = _dot(scores.astype(BF16), vh) + _dot(qd, state.astype(BF16))
        kd = (kTh.astype(F32) * kdec_ref[ql:ql + RET_QK_DIM, :]).astype(BF16)
        state_ref[hd] = state * sdec_ref[hd] + _dot(kd, vh)
        on = _rms_scale(o) * gn_ref[:, vl:vl + RET_V_DIM]
        y_ref[:, vl:vl + RET_V_DIM] = (gate_ref[:, vl:vl + RET_V_DIM].astype(F32) * on).astype(BF16)
    o_ref[...] = x_ref[...] + _dot(y_ref[...], wo_ref[...])


def _retention_tables():
    lg = jnp.log(1.0 - 2.0 ** (-5.0 - jnp.arange(RET_HEADS, dtype=F32)))
    pos = jnp.arange(SEQ_BLOCK, dtype=F32)
    chunk_of = jnp.arange(SEQ_BLOCK) // CHUNK
    visible = chunk_of[None, :] <= chunk_of[:, None]
    intra = jnp.exp(lg[:, None, None] * jnp.abs(pos[:, None] - pos[None, :]))
    intra = jnp.where(visible[None], intra, 0.0)
    q_dec = jnp.exp(lg[:, None] * (pos[None, :] + 1.0))
    k_dec = jnp.exp(lg[:, None] * (SEQ_BLOCK - 1.0 - pos[None, :]))
    s_dec = jnp.exp(lg * SEQ_BLOCK)
    qdec_full = jnp.repeat(q_dec.T, RET_QK_DIM, axis=1)
    kdec_full = jnp.repeat(k_dec, RET_QK_DIM, axis=0)
    sdec_full = jnp.broadcast_to(s_dec[:, None, None], (RET_HEADS, 1, RET_V_DIM))
    return qdec_full, kdec_full, intra, sdec_full


def _retention(q, kT, v, gate, xf, gn_g, w_o, batch, seq):
    t, d = xf.shape
    q_cols = RET_HEADS * RET_QK_DIM
    v_cols = RET_HEADS * RET_V_DIM
    nblk = seq // SEQ_BLOCK
    qdec, kdec, intra, sdec = _retention_tables()
    row_spec = lambda cols: pl.BlockSpec((SEQ_BLOCK, cols), lambda b, j: (b * nblk + j, 0))
    return pl.pallas_call(
        _retention_kernel,
        grid=(batch, nblk),
        in_specs=[
            row_spec(q_cols),
            pl.BlockSpec((1, q_cols, SEQ_BLOCK), lambda b, j: (b, 0, j)),
            row_spec(v_cols),
            row_spec(v_cols),
            row_spec(d),
            _resident((SEQ_BLOCK, q_cols)),
            _resident((q_cols, SEQ_BLOCK)),
            _resident((RET_HEADS, SEQ_BLOCK, SEQ_BLOCK)),
            _resident((RET_HEADS, 1, RET_V_DIM)),
            _resident((1, v_cols)),
            _resident((v_cols, d)),
        ],
        out_specs=row_spec(d),
        out_shape=jax.ShapeDtypeStruct((t, d), F32),
        scratch_shapes=[
            pltpu.VMEM((RET_HEADS, RET_QK_DIM, RET_V_DIM), F32),
            pltpu.VMEM((SEQ_BLOCK, v_cols), BF16),
        ],
        compiler_params=_params("parallel", "arbitrary"),
        name="retention",
    )(q, kT, v, gate, xf, qdec, kdec, intra, sdec, gn_g.reshape(1, v_cols), w_o.astype(BF16))


def _ffn_kernel(x_ref, g_ref, wgu_ref, wd_ref, o_ref, act_ref):
    x = x_ref[...]
    h = (_rms_scale(x) * g_ref[...]).astype(BF16)
    hidden = wd_ref.shape[0]
    for c in range(0, hidden, FFN_CHUNK):
        gate = _dot(h, wgu_ref[:, c:c + FFN_CHUNK])
        up = _dot(h, wgu_ref[:, hidden + c:hidden + c + FFN_CHUNK])
        act_ref[:, c:c + FFN_CHUNK] = (_silu(gate) * up).astype(BF16)
    o_ref[...] = x + _dot(act_ref[...], wd_ref[...])


def _ffn(xf, norm_g, w_gu, w_down):
    t, d = xf.shape
    hidden = w_down.shape[0]
    row_spec = pl.BlockSpec((ROW_TILE, d), lambda i: (i, 0))
    return pl.pallas_call(
        _ffn_kernel,
        grid=(t // ROW_TILE,),
        in_specs=[row_spec, _resident((1, d)), _resident((d, 2 * hidden)), _resident((hidden, d))],
        out_specs=row_spec,
        out_shape=jax.ShapeDtypeStruct((t, d), F32),
        scratch_shapes=[pltpu.VMEM((ROW_TILE, hidden), BF16)],
        compiler_params=_params("parallel"),
        name="ffn",
    )(xf, norm_g.reshape(1, d), w_gu.astype(BF16), w_down.astype(BF16))


def _kvq_kernel(x_ref, gkv_ref, gq_ref, wkT_ref, wv_ref, wq_ref, kg_ref,
                q_ref, kT_ref, v_ref):
    xn = _rms_scale(x_ref[...])
    u = (xn * gkv_ref[...]).astype(BF16)
    hq = (xn * gq_ref[...]).astype(BF16)
    d = x_ref.shape[1]
    for c in range(0, d, 512):
        v_ref[:, c:c + 512] = _dot(u, wv_ref[:, c:c + 512]).astype(BF16)
        q_ref[:, c:c + 512] = _dot(hq, wq_ref[:, c:c + 512]).astype(BF16)
    kg = kg_ref[...]
    for c in range(0, d, 256):
        kT = _dot_nt(wkT_ref[c:c + 256, :], u)
        for j in range(0, 256, ATT_HEAD_DIM):
            kh = kT[j:j + ATT_HEAD_DIM]
            inv = lax.rsqrt(jnp.mean(kh * kh, axis=0, keepdims=True) + EPS)
            kT_ref[0, c + j:c + j + ATT_HEAD_DIM, :] = (kh * inv * kg).astype(BF16)


def _kvq(xf, kv_norm_g, w_kv, k_norm_g, q_in_norm_g, w_q, batch, seq):
    t, d = xf.shape
    tiles_per_seq = seq // ROW_TILE
    row_spec = pl.BlockSpec((ROW_TILE, d), lambda i: (i, 0))
    return pl.pallas_call(
        _kvq_kernel,
        grid=(t // ROW_TILE,),
        in_specs=[
            row_spec, _resident((1, d)), _resident((1, d)),
            _resident((d, d)), _resident((d, d)), _resident((d, d)),
            _resident((ATT_HEAD_DIM, 1)),
        ],
        out_specs=[
            row_spec,
            pl.BlockSpec((1, d, ROW_TILE), lambda i: (i // tiles_per_seq, 0, i % tiles_per_seq)),
            row_spec,
        ],
        out_shape=[
            jax.ShapeDtypeStruct((t, d), BF16),
            jax.ShapeDtypeStruct((batch, d, seq), BF16),
            jax.ShapeDtypeStruct((t, d), BF16),
        ],
        compiler_params=_params("parallel"),
        name="kvq",
    )(xf, kv_norm_g.reshape(1, d), q_in_norm_g.reshape(1, d),
      w_kv[:, :d].T.astype(BF16), w_kv[:, d:].astype(BF16), w_q.astype(BF16),
      k_norm_g.reshape(ATT_HEAD_DIM, 1))


def _bias_table_kernel(rb_ref, tab_ref):
    lane = lax.broadcasted_iota(jnp.int32, (REL_PAD, BIAS_VEC), 1)
    entry = lax.broadcasted_iota(jnp.int32, (REL_PAD, BIAS_VEC), 0)
    key_minus_query = jnp.where(lane < PAIR_KEYS, lane, lane - BIAS_VEC)
    dist = PAST_CHUNKS * CHUNK - key_minus_query
    idx = jnp.clip(dist, -REL_CLIP, REL_CLIP) + REL_CLIP
    onehot = (entry == idx).astype(BF16)
    rb = rb_ref[...]
    hi = rb.astype(BF16)
    rest = rb - hi.astype(F32)
    mid = rest.astype(BF16)
    low = (rest - mid.astype(F32)).astype(BF16)
    vec = _dot(hi, onehot) + _dot(mid, onehot) + _dot(low, onehot)
    col = lax.broadcasted_iota(jnp.int32, (CHUNK, PAIR_KEYS), 1)
    for e in range(2):
        in_band = (col >= e * CHUNK) & (col < e * CHUNK + BAND)
        for h in range(ATT_HEADS):
            rows = jnp.broadcast_to(vec[h:h + 1, :], (CHUNK, BIAS_VEC))
            toeplitz = pltpu.roll(rows, e * CHUNK, 1, stride=1, stride_axis=0)
            tab_ref[h, e] = jnp.where(in_band, toeplitz[:, :PAIR_KEYS], -jnp.inf)


def _bias_table(rel_bias):
    heads, table = rel_bias.shape
    rb = jnp.pad(rel_bias, ((0, 0), (0, REL_PAD - table)))
    return pl.pallas_call(
        _bias_table_kernel,
        out_shape=jax.ShapeDtypeStruct((heads, 2, CHUNK, PAIR_KEYS), F32),
        compiler_params=pltpu.CompilerParams(vmem_limit_bytes=VMEM_LIMIT_BYTES),
        name="bias_table",
    )(rb)


_PAIR_PIECES = (
    ((0, 0, SEQ_BLOCK), (1, 0, SEQ_BLOCK), (2, 0, LANE)),
    ((0, LANE, LANE), (1, 0, SEQ_BLOCK), (2, 0, SEQ_BLOCK)),
)


SOFTMAX_LAG = 1
VALUE_LAG = 2
ATT_SLOTS = VALUE_LAG + 1


def _attention_kernel(q_ref, k0_ref, k1_ref, k2_ref, v0_ref, v1_ref, v2_ref, x_ref,
                      gq_ref, tab_ref, wo_ref, o_ref, qm_ref, s_ref, p_ref, rden_ref, ctx_ref):
    j = pl.program_id(1)
    k_refs = (k0_ref, k1_ref, k2_ref)
    v_refs = (v0_ref, v1_ref, v2_ref)
    pair_rows = 2 * CHUNK
    n_pairs = len(_PAIR_PIECES)
    block_mask = [jnp.where(j + p >= WINDOW_BLOCKS - 1, 0.0, -jnp.inf).astype(F32)
                  for p in range(WINDOW_BLOCKS - 1)]
    units = [(hp, pair, side) for hp in range(ATT_HEADS // 2)
             for pair in range(n_pairs) for side in range(2)]

    def first_head(rows):
        return lax.broadcasted_iota(jnp.int32, (rows, LANE), 1) < ATT_HEAD_DIM

    def scores(u):
        hp, pair, side = units[u]
        lo = hp * LANE
        if pair == 0 and side == 0:
            low = first_head(SEQ_BLOCK)
            qp = q_ref[:, lo:lo + LANE].astype(F32)
            sq = qp * qp
            ms0 = jnp.sum(jnp.where(low, sq, 0.0), axis=-1, keepdims=True) / ATT_HEAD_DIM
            ms1 = jnp.sum(jnp.where(low, 0.0, sq), axis=-1, keepdims=True) / ATT_HEAD_DIM
            inv = jnp.where(low, lax.rsqrt(ms0 + EPS), lax.rsqrt(ms1 + EPS))
            qn = qp * inv * gq_ref[:, lo:lo + LANE]
            qm_ref[2 * hp] = jnp.where(low, qn, 0.0).astype(BF16)
            qm_ref[2 * hp + 1] = jnp.where(low, 0.0, qn).astype(BF16)
        qrows = qm_ref[2 * hp + side, pair * pair_rows:(pair + 1) * pair_rows, :]
        col = 0
        for blk, c0, width in _PAIR_PIECES[pair]:
            sp = _dot(qrows, k_refs[blk][0, lo:lo + LANE, c0:c0 + width])
            sp = sp + tab_ref[2 * hp + side, :, col:col + width]
            if blk < WINDOW_BLOCKS - 1:
                sp = sp + block_mask[blk]
            s_ref[u % ATT_SLOTS, :, col:col + width] = sp
            col += width

    def softmax(u):
        slot = u % ATT_SLOTS
        for e in range(2):
            rows = slice(e * CHUNK, (e + 1) * CHUNK)
            cols = range(0, PAIR_KEYS, LANE)
            m = jnp.max(functools.reduce(jnp.maximum, [s_ref[slot, rows, i:i + LANE] for i in cols]),
                        axis=-1, keepdims=True)
            total = None
            for i in cols:
                ex = jnp.exp(s_ref[slot, rows, i:i + LANE] - m)
                total = ex if total is None else total + ex
                p_ref[slot, rows, i:i + LANE] = ex.astype(BF16)
            denom = jnp.sum(total, axis=-1, keepdims=True)
            rden_ref[slot, rows, :] = jnp.broadcast_to(1.0 / denom, (CHUNK, LANE))

    def values(u):
        hp, pair, side = units[u]
        slot = u % ATT_SLOTS
        lo = hp * LANE
        rows = slice(pair * pair_rows, (pair + 1) * pair_rows)
        ctx = None
        col = 0
        for blk, c0, width in _PAIR_PIECES[pair]:
            part = _dot(p_ref[slot, :, col:col + width], v_refs[blk][c0:c0 + width, lo:lo + LANE])
            ctx = part if ctx is None else ctx + part
            col += width
        ctx = ctx * rden_ref[slot]
        if side == 0:
            ctx_ref[rows, lo:lo + LANE] = ctx.astype(BF16)
        else:
            kept = ctx_ref[rows, lo:lo + LANE].astype(F32)
            ctx_ref[rows, lo:lo + LANE] = jnp.where(first_head(pair_rows), kept, ctx).astype(BF16)

    for step in range(len(units) + VALUE_LAG):
        if step < len(units):
            scores(step)
        if 0 <= step - SOFTMAX_LAG < len(units):
            softmax(step - SOFTMAX_LAG)
        if step >= VALUE_LAG:
            values(step - VALUE_LAG)
    o_ref[...] = x_ref[...] + _dot(ctx_ref[...], wo_ref[...])


def _attention(q, kT, v, xf, q_norm_g, table, w_o, batch, seq):
    t, d = xf.shape
    nblk = seq // SEQ_BLOCK
    back = WINDOW_BLOCKS - 1
    gq = jnp.tile(q_norm_g * (ATT_HEAD_DIM ** -0.5), ATT_HEADS).reshape(1, d)
    row_spec = pl.BlockSpec((SEQ_BLOCK, d), lambda b, j: (b * nblk + j, 0))

    def k_spec(p):
        return pl.BlockSpec((1, d, SEQ_BLOCK),
                            lambda b, j: (b, 0, jnp.maximum(j - back + p, 0)))

    def v_spec(p):
        return pl.BlockSpec((SEQ_BLOCK, d),
                            lambda b, j: (b * nblk + jnp.maximum(j - back + p, 0), 0))

    return pl.pallas_call(
        _attention_kernel,
        grid=(batch, nblk),
        in_specs=[row_spec]
                 + [k_spec(p) for p in range(WINDOW_BLOCKS)]
                 + [v_spec(p) for p in range(WINDOW_BLOCKS)]
                 + [row_spec, _resident((1, d)),
                    _resident((ATT_HEADS, 2 * CHUNK, PAIR_KEYS)), _resident((d, d))],
        out_specs=row_spec,
        out_shape=jax.ShapeDtypeStruct((t, d), F32),
        scratch_shapes=[pltpu.VMEM((ATT_HEADS, SEQ_BLOCK, LANE), BF16),
                        pltpu.VMEM((ATT_SLOTS, 2 * CHUNK, PAIR_KEYS), F32),
                        pltpu.VMEM((ATT_SLOTS, 2 * CHUNK, PAIR_KEYS), BF16),
                        pltpu.VMEM((ATT_SLOTS, 2 * CHUNK, LANE), F32),
                        pltpu.VMEM((SEQ_BLOCK, d), BF16)],
        compiler_params=_params("parallel", "arbitrary"),
        name="attention",
    )(q, kT, kT, kT, v, v, v, xf, gq,
      table.reshape(ATT_HEADS, 2 * CHUNK, PAIR_KEYS), w_o.astype(BF16))


def kernel(x, a_norm_g, a_w_in, a_gn_g, a_w_o, a_ffn_norm_g, a_w_gu, a_w_down,
           kv_norm_g, w_kv, k_norm_g,
           b_norm_g, b_w_q, b_q_norm_g, b_rel_bias, b_w_o, b_ffn_norm_g, b_w_gu, b_w_down):
    batch, seq, d = x.shape
    assert seq % ROW_TILE == 0 and seq % SEQ_BLOCK == 0
    assert WINDOW_BLOCKS * CHUNKS_PER_BLOCK >= PAST_CHUNKS + CHUNKS_PER_BLOCK
    assert b_norm_g.shape[0] == 1, "the query projection is fused with the shared K/V projection"
    xf = x.reshape(batch * seq, d)
    for i in range(a_norm_g.shape[0]):
        q, kT, v, gate = _inproj(xf, a_norm_g[i], a_w_in[i], batch, seq)
        xf = _retention(q, kT, v, gate, xf, a_gn_g[i], a_w_o[i], batch, seq)
        xf = _ffn(xf, a_ffn_norm_g[i], a_w_gu[i], a_w_down[i])
    q, kT, v = _kvq(xf, kv_norm_g, w_kv, k_norm_g, b_norm_g[0], b_w_q[0], batch, seq)
    table = _bias_table(b_rel_bias[0])
    xf = _attention(q, kT, v, xf, b_q_norm_g[0], table, b_w_o[0], batch, seq)
    xf = _ffn(xf, b_ffn_norm_g[0], b_w_gu[0], b_w_down[0])
    return xf.reshape(batch, seq, d)
```

```python
import functools

import jax
import jax.numpy as jnp
from jax import lax
from jax.experimental import pallas as pl
from jax.experimental.pallas import tpu as pltpu

F32 = jnp.float32
BF16 = jnp.bfloat16

CHUNK = 64
EPS = 1e-6
RET_HEADS = 4
RET_QK_DIM = 256
RET_V_DIM = 512
ROPE_BASE = 10000.0
ATT_HEADS = 16
ATT_HEAD_DIM = 64
PAST_CHUNKS = 8
REL_CLIP = 256
LOG2_E = 1.4426950408889634

ROW_TILE = 512
SEQ_BLOCK = 256
CHUNKS_PER_BLOCK = SEQ_BLOCK // CHUNK
WINDOW_BLOCKS = 1 + PAST_CHUNKS // CHUNKS_PER_BLOCK
LANE = 128
BAND = (PAST_CHUNKS + 1) * CHUNK
PAIR_KEYS = BAND + CHUNK
BIAS_VEC = 1024
REL_PAD = 640
FFN_CHUNK = 256
VMEM_LIMIT_BYTES = 56 * 1024 * 1024


def _resident(shape):
    zeros = (0,) * len(shape)
    return pl.BlockSpec(shape, lambda *_: zeros, pipeline_mode=pl.Buffered(1))


def _params(*semantics):
    return pltpu.CompilerParams(dimension_semantics=semantics,
                                vmem_limit_bytes=VMEM_LIMIT_BYTES)


def _rms_scale(x):
    return x * lax.rsqrt(jnp.mean(x * x, axis=-1, keepdims=True) + EPS)


def _dot(a, b):
    return jnp.dot(a, b, preferred_element_type=F32)


def _dot_nt(a, b):
    return lax.dot_general(a, b, (((1,), (1,)), ((), ())), preferred_element_type=F32)


def _silu(x):
    return x * jax.nn.sigmoid(x)


def _store_seq_blocks(ref, row_lo, value):
    for blk in range(ROW_TILE // SEQ_BLOCK):
        ref[0, blk, row_lo:row_lo + value.shape[0], :] = value[:, blk * SEQ_BLOCK:(blk + 1) * SEQ_BLOCK]


def _inproj_kernel(x_ref, g_ref, wq_ref, wkT_ref, wv_ref, wg_ref,
                   cos_ref, sin_ref, cosT_ref, sinT_ref,
                   q_ref, kT_ref, v_ref, gate_ref):
    h = (_rms_scale(x_ref[...]) * g_ref[...]).astype(BF16)
    cos, sin = cos_ref[...], sin_ref[...]
    cosT, sinT = cosT_ref[...], sinT_ref[...]
    half = RET_QK_DIM // 2
    k_scale = RET_QK_DIM ** -0.5
    for hd in range(RET_HEADS):
        lo = hd * RET_QK_DIM
        qh = _dot(h, wq_ref[:, lo:lo + RET_QK_DIM])
        x1, x2 = qh[:, :half], qh[:, half:]
        q_ref[:, lo:lo + half] = (x1 * cos - x2 * sin).astype(BF16)
        q_ref[:, lo + half:lo + RET_QK_DIM] = (x1 * sin + x2 * cos).astype(BF16)
        kh = _dot_nt(wkT_ref[lo:lo + RET_QK_DIM, :], h)
        y1, y2 = kh[:half], kh[half:]
        _store_seq_blocks(kT_ref, lo, ((y1 * cosT - y2 * sinT) * k_scale).astype(BF16))
        _store_seq_blocks(kT_ref, lo + half, ((y1 * sinT + y2 * cosT) * k_scale).astype(BF16))
    for hd in range(RET_HEADS):
        lo = hd * RET_V_DIM
        v_ref[:, lo:lo + RET_V_DIM] = _dot(h, wv_ref[:, lo:lo + RET_V_DIM]).astype(BF16)
        gate_ref[:, lo:lo + RET_V_DIM] = _silu(_dot(h, wg_ref[:, lo:lo + RET_V_DIM])).astype(BF16)


def _inproj(xf, norm_g, w_in, batch, seq):
    t, d = xf.shape
    q_cols = RET_HEADS * RET_QK_DIM
    v_cols = RET_HEADS * RET_V_DIM
    half = RET_QK_DIM // 2
    wq = w_in[:, :q_cols].astype(BF16)
    wkT = w_in[:, q_cols:2 * q_cols].T.astype(BF16)
    wv = w_in[:, 2 * q_cols:2 * q_cols + v_cols].astype(BF16)
    wg = w_in[:, 2 * q_cols + v_cols:].astype(BF16)
    inv_freq = ROPE_BASE ** (-jnp.arange(half, dtype=F32) / half)
    ang = jnp.arange(seq).astype(F32)[:, None] * inv_freq[None, :]
    cos, sin = jnp.cos(ang), jnp.sin(ang)
    tiles_per_seq = seq // ROW_TILE
    row_spec = lambda cols: pl.BlockSpec((ROW_TILE, cols), lambda i: (i, 0))
    return pl.pallas_call(
        _inproj_kernel,
        grid=(t // ROW_TILE,),
        in_specs=[
            row_spec(d),
            _resident((1, d)),
            _resident((d, q_cols)),
            _resident((q_cols, d)),
            _resident((d, v_cols)),
            _resident((d, v_cols)),
            pl.BlockSpec((ROW_TILE, half), lambda i: (i % tiles_per_seq, 0)),
            pl.BlockSpec((ROW_TILE, half), lambda i: (i % tiles_per_seq, 0)),
            pl.BlockSpec((half, ROW_TILE), lambda i: (0, i % tiles_per_seq)),
            pl.BlockSpec((half, ROW_TILE), lambda i: (0, i % tiles_per_seq)),
        ],
        out_specs=[
            row_spec(q_cols),
            pl.BlockSpec((1, ROW_TILE // SEQ_BLOCK, q_cols, SEQ_BLOCK),
                         lambda i: (i // tiles_per_seq, i % tiles_per_seq, 0, 0)),
            row_spec(v_cols),
            row_spec(v_cols),
        ],
        out_shape=[
            jax.ShapeDtypeStruct((t, q_cols), BF16),
            jax.ShapeDtypeStruct((batch, seq // SEQ_BLOCK, q_cols, SEQ_BLOCK), BF16),
            jax.ShapeDtypeStruct((t, v_cols), BF16),
            jax.ShapeDtypeStruct((t, v_cols), BF16),
        ],
        compiler_params=_params("parallel"),
        name="inproj",
    )(xf, norm_g.reshape(1, d), wq, wkT, wv, wg, cos, sin, cos.T, sin.T)


def _retention_kernel(q_ref, kT_ref, v_ref, gate_ref, x_ref,
                      qdec_ref, kdec_ref, intra_ref, sdec_ref, gn_ref, wo_ref,
                      o_ref, state_ref, y_ref):
    @pl.when(pl.program_id(1) == 0)
    def _():
        state_ref[...] = jnp.zeros_like(state_ref)

    for hd in range(RET_HEADS):
        ql, vl = hd * RET_QK_DIM, hd * RET_V_DIM
        qh = q_ref[:, ql:ql + RET_QK_DIM]
        kTh = kT_ref[0, 0, ql:ql + RET_QK_DIM, :]
        vh = v_ref[:, vl:vl + RET_V_DIM]
        state = state_ref[hd]
        scores = _dot(qh, kTh)
        qd = (qh.astype(F32) * qdec_ref[:, ql:ql + RET_QK_DIM]).astype(BF16)
        cross = _dot(qd, state.astype(BF16))
        kd = (kTh.astype(F32) * kdec_ref[ql:ql + RET_QK_DIM, :]).astype(BF16)
        state_ref[hd] = state * sdec_ref[hd] + _dot(kd, vh)
        o = _dot((scores * intra_ref[hd]).astype(BF16), vh) + cross
        on = _rms_scale(o) * gn_ref[:, vl:vl + RET_V_DIM]
        y_ref[:, vl:vl + RET_V_DIM] = (gate_ref[:, vl:vl + RET_V_DIM].astype(F32) * on).astype(BF16)
    o_ref[...] = x_ref[...] + _dot(y_ref[...], wo_ref[...])


def _retention_tables():
    lg = jnp.log(1.0 - 2.0 ** (-5.0 - jnp.arange(RET_HEADS, dtype=F32)))
    pos = jnp.arange(SEQ_BLOCK, dtype=F32)
    chunk_of = jnp.arange(SEQ_BLOCK) // CHUNK
    visible = chunk_of[None, :] <= chunk_of[:, None]
    intra = jnp.exp(lg[:, None, None] * jnp.abs(pos[:, None] - pos[None, :]))
    intra = jnp.where(visible[None], intra, 0.0)
    q_dec = jnp.exp(lg[:, None] * (pos[None, :] + 1.0))
    k_dec = jnp.exp(lg[:, None] * (SEQ_BLOCK - 1.0 - pos[None, :]))
    s_dec = jnp.exp(lg * SEQ_BLOCK)
    qdec_full = jnp.repeat(q_dec.T, RET_QK_DIM, axis=1)
    kdec_full = jnp.repeat(k_dec, RET_QK_DIM, axis=0)
    sdec_full = jnp.broadcast_to(s_dec[:, None, None], (RET_HEADS, 1, RET_V_DIM))
    return qdec_full, kdec_full, intra, sdec_full


def _retention(q, kT, v, gate, xf, gn_g, w_o, batch, seq):
    t, d = xf.shape
    q_cols = RET_HEADS * RET_QK_DIM
    v_cols = RET_HEADS * RET_V_DIM
    nblk = seq // SEQ_BLOCK
    qdec, kdec, intra, sdec = _retention_tables()
    row_spec = lambda cols: pl.BlockSpec((SEQ_BLOCK, cols), lambda b, j: (b * nblk + j, 0))
    return pl.pallas_call(
        _retention_kernel,
        grid=(batch, nblk),
        in_specs=[
            row_spec(q_cols),
            pl.BlockSpec((1, 1, q_cols, SEQ_BLOCK), lambda b, j: (b, j, 0, 0)),
            row_spec(v_cols),
            row_spec(v_cols),
            row_spec(d),
            _resident((SEQ_BLOCK, q_cols)),
            _resident((q_cols, SEQ_BLOCK)),
            _resident((RET_HEADS, SEQ_BLOCK, SEQ_BLOCK)),
            _resident((RET_HEADS, 1, RET_V_DIM)),
            _resident((1, v_cols)),
            _resident((v_cols, d)),
        ],
        out_specs=row_spec(d),
        out_shape=jax.ShapeDtypeStruct((t, d), F32),
        scratch_shapes=[
            pltpu.VMEM((RET_HEADS, RET_QK_DIM, RET_V_DIM), F32),
            pltpu.VMEM((SEQ_BLOCK, v_cols), BF16),
        ],
        compiler_params=_params("parallel", "arbitrary"),
        name="retention",
    )(q, kT, v, gate, xf, qdec, kdec, intra, sdec, gn_g.reshape(1, v_cols), w_o.astype(BF16))


def _ffn_kernel(x_ref, g_ref, wgu_ref, wd_ref, o_ref, act_ref):
    x = x_ref[...]
    h = (_rms_scale(x) * g_ref[...]).astype(BF16)
    hidden = wd_ref.shape[0]
    for c in range(0, hidden, FFN_CHUNK):
        gate = _dot(h, wgu_ref[:, c:c + FFN_CHUNK])
        up = _dot(h, wgu_ref[:, hidden + c:hidden + c + FFN_CHUNK])
        act_ref[:, c:c + FFN_CHUNK] = (_silu(gate) * up).astype(BF16)
    o_ref[...] = x + _dot(act_ref[...], wd_ref[...])


def _ffn(xf, norm_g, w_gu, w_down):
    t, d = xf.shape
    hidden = w_down.shape[0]
    row_spec = pl.BlockSpec((ROW_TILE, d), lambda i: (i, 0))
    return pl.pallas_call(
        _ffn_kernel,
        grid=(t // ROW_TILE,),
        in_specs=[row_spec, _resident((1, d)), _resident((d, 2 * hidden)), _resident((hidden, d))],
        out_specs=row_spec,
        out_shape=jax.ShapeDtypeStruct((t, d), F32),
        scratch_shapes=[pltpu.VMEM((ROW_TILE, hidden), BF16)],
        compiler_params=_params("parallel"),
        name="ffn",
    )(xf, norm_g.reshape(1, d), w_gu.astype(BF16), w_down.astype(BF16))


def _kvq_kernel(x_ref, gkv_ref, gq_ref, wkT_ref, wv_ref, wq_ref, kg_ref,
                q_ref, kT_ref, v_ref):
    xn = _rms_scale(x_ref[...])
    u = (xn * gkv_ref[...]).astype(BF16)
    hq = (xn * gq_ref[...]).astype(BF16)
    d = x_ref.shape[1]
    first_head = lax.broadcasted_iota(jnp.int32, (x_ref.shape[0], LANE), 1) < ATT_HEAD_DIM
    for c in range(0, d, 512):
        q_ref[:, c:c + 512] = _dot(hq, wq_ref[:, c:c + 512]).astype(BF16)
        vv = _dot(u, wv_ref[:, c:c + 512])
        for i in range(0, 512, LANE):
            pair_v = vv[:, i:i + LANE]
            v_ref[:, 2 * (c + i):2 * (c + i) + LANE] = jnp.where(first_head, pair_v, 1.0).astype(BF16)
            v_ref[:, 2 * (c + i) + LANE:2 * (c + i + LANE)] = jnp.where(first_head, 1.0, pair_v).astype(BF16)
    kg = kg_ref[...]
    for c in range(0, d, 256):
        kT = _dot_nt(wkT_ref[c:c + 256, :], u)
        for j in range(0, 256, ATT_HEAD_DIM):
            kh = kT[j:j + ATT_HEAD_DIM]
            inv = lax.rsqrt(jnp.mean(kh * kh, axis=0, keepdims=True) + EPS)
            _store_seq_blocks(kT_ref, c + j, (kh * inv * kg).astype(BF16))


def _kvq(xf, kv_norm_g, w_kv, k_norm_g, q_in_norm_g, w_q, batch, seq):
    t, d = xf.shape
    tiles_per_seq = seq // ROW_TILE
    row_spec = pl.BlockSpec((ROW_TILE, d), lambda i: (i, 0))
    return pl.pallas_call(
        _kvq_kernel,
        grid=(t // ROW_TILE,),
        in_specs=[
            row_spec, _resident((1, d)), _resident((1, d)),
            _resident((d, d)), _resident((d, d)), _resident((d, d)),
            _resident((ATT_HEAD_DIM, 1)),
        ],
        out_specs=[
            row_spec,
            pl.BlockSpec((1, ROW_TILE // SEQ_BLOCK, d, SEQ_BLOCK),
                         lambda i: (i // tiles_per_seq, i % tiles_per_seq, 0, 0)),
            pl.BlockSpec((ROW_TILE, 2 * d), lambda i: (i, 0)),
        ],
        out_shape=[
            jax.ShapeDtypeStruct((t, d), BF16),
            jax.ShapeDtypeStruct((batch, seq // SEQ_BLOCK, d, SEQ_BLOCK), BF16),
            jax.ShapeDtypeStruct((t, 2 * d), BF16),
        ],
        compiler_params=_params("parallel"),
        name="kvq",
    )(xf, kv_norm_g.reshape(1, d), q_in_norm_g.reshape(1, d),
      w_kv[:, :d].T.astype(BF16), w_kv[:, d:].astype(BF16), w_q.astype(BF16),
      k_norm_g.reshape(ATT_HEAD_DIM, 1))


def _bias_table_kernel(rb_ref, tab_ref):
    lane = lax.broadcasted_iota(jnp.int32, (REL_PAD, BIAS_VEC), 1)
    entry = lax.broadcasted_iota(jnp.int32, (REL_PAD, BIAS_VEC), 0)
    key_minus_query = jnp.where(lane < PAIR_KEYS, lane, lane - BIAS_VEC)
    dist = PAST_CHUNKS * CHUNK - key_minus_query
    idx = jnp.clip(dist, -REL_CLIP, REL_CLIP) + REL_CLIP
    onehot = (entry == idx).astype(BF16)
    rb = rb_ref[...]
    hi = rb.astype(BF16)
    rest = rb - hi.astype(F32)
    mid = rest.astype(BF16)
    low = (rest - mid.astype(F32)).astype(BF16)
    vec = (_dot(hi, onehot) + _dot(mid, onehot) + _dot(low, onehot)) * LOG2_E
    col = lax.broadcasted_iota(jnp.int32, (CHUNK, PAIR_KEYS), 1)
    for e in range(2):
        in_band = (col >= e * CHUNK) & (col < e * CHUNK + BAND)
        for h in range(ATT_HEADS):
            rows = jnp.broadcast_to(vec[h:h + 1, :], (CHUNK, BIAS_VEC))
            toeplitz = pltpu.roll(rows, e * CHUNK, 1, stride=1, stride_axis=0)
            tab_ref[h, e] = jnp.where(in_band, toeplitz[:, :PAIR_KEYS], -jnp.inf)


def _bias_table(rel_bias):
    heads, table = rel_bias.shape
    rb = jnp.pad(rel_bias, ((0, 0), (0, REL_PAD - table)))
    return pl.pallas_call(
        _bias_table_kernel,
        out_shape=jax.ShapeDtypeStruct((heads, 2, CHUNK, PAIR_KEYS), F32),
        compiler_params=pltpu.CompilerParams(vmem_limit_bytes=VMEM_LIMIT_BYTES),
        name="bias_table",
    )(rb)


_PAIR_PIECES = (
    ((0, 0, SEQ_BLOCK), (1, 0, SEQ_BLOCK), (2, 0, LANE)),
    ((0, LANE, LANE), (1, 0, SEQ_BLOCK), (2, 0, SEQ_BLOCK)),
)


SOFTMAX_LAG = 1
VALUE_LAG = 2
ATT_SLOTS = VALUE_LAG + 1


def _attention_kernel(q_ref, k0_ref, k1_ref, k2_ref, v0_ref, v1_ref, v2_ref, x_ref,
                      gq_ref, tab_ref, wo_ref, o_ref, qm_ref, s_ref, mx_ref, p_ref, ctx_ref):
    j = pl.program_id(1)
    k_refs = (k0_ref, k1_ref, k2_ref)
    v_refs = (v0_ref, v1_ref, v2_ref)
    pair_rows = 2 * CHUNK
    units = [(hp, pair, side) for hp in range(ATT_HEADS // 2)
             for pair in range(len(_PAIR_PIECES)) for side in range(2)]

    def first_head(rows):
        return lax.broadcasted_iota(jnp.int32, (rows, LANE), 1) < ATT_HEAD_DIM

    def scores(u, block_mask):
        hp, pair, side = units[u]
        lo = hp * LANE
        if pair == 0 and side == 0:
            low = first_head(SEQ_BLOCK)
            qp = q_ref[:, lo:lo + LANE].astype(F32)
            sq = qp * qp
            ms0 = jnp.sum(jnp.where(low, sq, 0.0), axis=-1, keepdims=True) / ATT_HEAD_DIM
            ms1 = jnp.sum(jnp.where(low, 0.0, sq), axis=-1, keepdims=True) / ATT_HEAD_DIM
            inv = jnp.where(low, lax.rsqrt(ms0 + EPS), lax.rsqrt(ms1 + EPS))
            qn = qp * inv * gq_ref[:, lo:lo + LANE]
            qm_ref[2 * hp] = jnp.where(low, qn, 0.0).astype(BF16)
            qm_ref[2 * hp + 1] = jnp.where(low, 0.0, qn).astype(BF16)
        qrows = qm_ref[2 * hp + side, pair * pair_rows:(pair + 1) * pair_rows, :]
        slot = u % ATT_SLOTS
        col = 0
        running_max = None
        for blk, c0, width in _PAIR_PIECES[pair]:
            sp = _dot(qrows, k_refs[blk][0, 0, lo:lo + LANE, c0:c0 + width])
            sp = sp + tab_ref[2 * hp + side, :, col:col + width]
            if block_mask is not None and blk < WINDOW_BLOCKS - 1:
                sp = sp + block_mask[blk]
            s_ref[slot, :, col:col + width] = sp
            for i in range(0, width, LANE):
                tile = sp[:, i:i + LANE]
                running_max = tile if running_max is None else jnp.maximum(running_max, tile)
            col += width
        mx_ref[slot] = running_max

    def softmax(u):
        slot = u % ATT_SLOTS
        for e in range(2):
            rows = slice(e * CHUNK, (e + 1) * CHUNK)
            m = jnp.max(mx_ref[slot, rows, :], axis=-1, keepdims=True)
            for i in range(0, PAIR_KEYS, LANE):
                p_ref[slot, rows, i:i + LANE] = jnp.exp2((s_ref[slot, rows, i:i + LANE] - m).astype(BF16))

    def values(u):
        hp, pair, side = units[u]
        slot = u % ATT_SLOTS
        rows = slice(pair * pair_rows, (pair + 1) * pair_rows)
        lo = (2 * hp + side) * LANE
        acc = None
        col = 0
        for blk, c0, width in _PAIR_PIECES[pair]:
            part = _dot(p_ref[slot, :, col:col + width], v_refs[blk][c0:c0 + width, lo:lo + LANE])
            acc = part if acc is None else acc + part
            col += width
        ctx = acc / pltpu.roll(acc, ATT_HEAD_DIM, 1)
        out_lo = hp * LANE
        if side == 0:
            ctx_ref[rows, out_lo:out_lo + LANE] = ctx.astype(BF16)
        else:
            kept = ctx_ref[rows, out_lo:out_lo + LANE].astype(F32)
            ctx_ref[rows, out_lo:out_lo + LANE] = jnp.where(
                first_head(pair_rows), kept, ctx).astype(BF16)

    def all_units(block_mask):
        for step in range(len(units) + VALUE_LAG):
            if step < len(units):
                scores(step, block_mask)
            if 0 <= step - SOFTMAX_LAG < len(units):
                softmax(step - SOFTMAX_LAG)
            if step >= VALUE_LAG:
                values(step - VALUE_LAG)

    @pl.when(j >= WINDOW_BLOCKS - 1)
    def _():
        all_units(None)

    @pl.when(j < WINDOW_BLOCKS - 1)
    def _():
        all_units([jnp.where(j + p >= WINDOW_BLOCKS - 1, 0.0, -jnp.inf).astype(F32)
                   for p in range(WINDOW_BLOCKS - 1)])

    o_ref[...] = x_ref[...] + _dot(ctx_ref[...], wo_ref[...])


def _attention(q, kT, v, xf, q_norm_g, table, w_o, batch, seq):
    t, d = xf.shape
    nblk = seq // SEQ_BLOCK
    back = WINDOW_BLOCKS - 1
    gq = jnp.tile(q_norm_g * (ATT_HEAD_DIM ** -0.5 * LOG2_E), ATT_HEADS).reshape(1, d)
    row_spec = pl.BlockSpec((SEQ_BLOCK, d), lambda b, j: (b * nblk + j, 0))

    def k_spec(p):
        return pl.BlockSpec((1, 1, d, SEQ_BLOCK),
                            lambda b, j: (b, jnp.maximum(j - back + p, 0), 0, 0))

    def v_spec(p):
        return pl.BlockSpec((SEQ_BLOCK, 2 * d),
                            lambda b, j: (b * nblk + jnp.maximum(j - back + p, 0), 0))

    return pl.pallas_call(
        _attention_kernel,
        grid=(batch, nblk),
        in_specs=[row_spec]
                 + [k_spec(p) for p in range(WINDOW_BLOCKS)]
                 + [v_spec(p) for p in range(WINDOW_BLOCKS)]
                 + [row_spec, _resident((1, d)),
                    _resident((ATT_HEADS, 2 * CHUNK, PAIR_KEYS)), _resident((d, d))],
        out_specs=row_spec,
        out_shape=jax.ShapeDtypeStruct((t, d), F32),
        scratch_shapes=[pltpu.VMEM((ATT_HEADS, SEQ_BLOCK, LANE), BF16),
                        pltpu.VMEM((ATT_SLOTS, 2 * CHUNK, PAIR_KEYS), F32),
                        pltpu.VMEM((ATT_SLOTS, 2 * CHUNK, LANE), F32),
                        pltpu.VMEM((ATT_SLOTS, 2 * CHUNK, PAIR_KEYS), BF16),
                        pltpu.VMEM((SEQ_BLOCK, d), BF16)],
        compiler_params=_params("parallel", "arbitrary"),
        name="attention",
    )(q, kT, kT, kT, v, v, v, xf, gq,
      table.reshape(ATT_HEADS, 2 * CHUNK, PAIR_KEYS), w_o.astype(BF16))


def kernel(x, a_norm_g, a_w_in, a_gn_g, a_w_o, a_ffn_norm_g, a_w_gu, a_w_down,
           kv_norm_g, w_kv, k_norm_g,
           b_norm_g, b_w_q, b_q_norm_g, b_rel_bias, b_w_o, b_ffn_norm_g, b_w_gu, b_w_down):
    batch, seq, d = x.shape
    assert seq % ROW_TILE == 0 and seq % SEQ_BLOCK == 0
    assert WINDOW_BLOCKS * CHUNKS_PER_BLOCK >= PAST_CHUNKS + CHUNKS_PER_BLOCK
    assert b_norm_g.shape[0] == 1, "the query projection is fused with the shared K/V projection"
    xf = x.reshape(batch * seq, d)
    for i in range(a_norm_g.shape[0]):
        q, kT, v, gate = _inproj(xf, a_norm_g[i], a_w_in[i], batch, seq)
        xf = _retention(q, kT, v, gate, xf, a_gn_g[i], a_w_o[i], batch, seq)
        xf = _ffn(xf, a_ffn_norm_g[i], a_w_gu[i], a_w_down[i])
    q, kT, v = _kvq(xf, kv_norm_g, w_kv, k_norm_g, b_norm_g[0], b_w_q[0], batch, seq)
    table = _bias_table(b_rel_bias[0])
    xf = _attention(q, kT, v, xf, b_q_norm_g[0], table, b_w_o[0], batch, seq)
    xf = _ffn(xf, b_ffn_norm_g[0], b_w_gu[0], b_w_down[0])
    return xf.reshape(batch, seq, d)
```

```python
import functools

import jax
import jax.numpy as jnp
from jax import lax
from jax.experimental import pallas as pl
from jax.experimental.pallas import tpu as pltpu

F32 = jnp.float32
BF16 = jnp.bfloat16

CHUNK = 64
EPS = 1e-6
RET_HEADS = 4
RET_QK_DIM = 256
RET_V_DIM = 512
ROPE_BASE = 10000.0
ATT_HEADS = 16
ATT_HEAD_DIM = 64
PAST_CHUNKS = 8
REL_CLIP = 256
LOG2_E = 1.4426950408889634

ROW_TILE = 512
FFN_ROW_TILE = 1024
SEQ_BLOCK = 256
CHUNKS_PER_BLOCK = SEQ_BLOCK // CHUNK
WINDOW_BLOCKS = 1 + PAST_CHUNKS // CHUNKS_PER_BLOCK
LANE = 128
BAND = (PAST_CHUNKS + 1) * CHUNK
PAIR_KEYS = BAND + CHUNK
BIAS_VEC = 1024
REL_PAD = 640
FFN_CHUNK = 256
VMEM_LIMIT_BYTES = 56 * 1024 * 1024


def _resident(shape):
    zeros = (0,) * len(shape)
    return pl.BlockSpec(shape, lambda *_: zeros, pipeline_mode=pl.Buffered(1))


def _params(*semantics):
    return pltpu.CompilerParams(dimension_semantics=semantics,
                                vmem_limit_bytes=VMEM_LIMIT_BYTES)


def _rms_scale(x):
    return x * lax.rsqrt(jnp.mean(x * x, axis=-1, keepdims=True) + EPS)


def _dot(a, b):
    return jnp.dot(a, b, preferred_element_type=F32)


def _dot_nt(a, b):
    return lax.dot_general(a, b, (((1,), (1,)), ((), ())), preferred_element_type=F32)


def _silu(x):
    return x * jax.nn.sigmoid(x)


def _store_seq_blocks(ref, row_lo, value):
    for blk in range(ROW_TILE // SEQ_BLOCK):
        ref[0, blk, row_lo:row_lo + value.shape[0], :] = value[:, blk * SEQ_BLOCK:(blk + 1) * SEQ_BLOCK]


def _inproj_kernel(x_ref, g_ref, wq_ref, wkT_ref, wv_ref, wg_ref,
                   cos_ref, sin_ref, cosT_ref, sinT_ref,
                   q_ref, kT_ref, v_ref, gate_ref):
    h = (_rms_scale(x_ref[...]) * g_ref[...]).astype(BF16)
    cos, sin = cos_ref[...], sin_ref[...]
    cosT, sinT = cosT_ref[...], sinT_ref[...]
    half = RET_QK_DIM // 2
    k_scale = RET_QK_DIM ** -0.5
    for hd in range(RET_HEADS):
        lo = hd * RET_QK_DIM
        qh = _dot(h, wq_ref[:, lo:lo + RET_QK_DIM])
        x1, x2 = qh[:, :half], qh[:, half:]
        q_ref[:, lo:lo + half] = (x1 * cos - x2 * sin).astype(BF16)
        q_ref[:, lo + half:lo + RET_QK_DIM] = (x1 * sin + x2 * cos).astype(BF16)
        kh = _dot_nt(wkT_ref[lo:lo + RET_QK_DIM, :], h)
        y1, y2 = kh[:half], kh[half:]
        _store_seq_blocks(kT_ref, lo, ((y1 * cosT - y2 * sinT) * k_scale).astype(BF16))
        _store_seq_blocks(kT_ref, lo + half, ((y1 * sinT + y2 * cosT) * k_scale).astype(BF16))
    for hd in range(RET_HEADS):
        lo = hd * RET_V_DIM
        v_ref[:, lo:lo + RET_V_DIM] = _dot(h, wv_ref[:, lo:lo + RET_V_DIM]).astype(BF16)
        gate_ref[:, lo:lo + RET_V_DIM] = _silu(_dot(h, wg_ref[:, lo:lo + RET_V_DIM])).astype(BF16)


def _inproj(xf, norm_g, w_in, batch, seq):
    t, d = xf.shape
    q_cols = RET_HEADS * RET_QK_DIM
    v_cols = RET_HEADS * RET_V_DIM
    half = RET_QK_DIM // 2
    wq = w_in[:, :q_cols].astype(BF16)
    wkT = w_in[:, q_cols:2 * q_cols].T.astype(BF16)
    wv = w_in[:, 2 * q_cols:2 * q_cols + v_cols].astype(BF16)
    wg = w_in[:, 2 * q_cols + v_cols:].astype(BF16)
    inv_freq = ROPE_BASE ** (-jnp.arange(half, dtype=F32) / half)
    ang = jnp.arange(seq).astype(F32)[:, None] * inv_freq[None, :]
    cos, sin = jnp.cos(ang), jnp.sin(ang)
    tiles_per_seq = seq // ROW_TILE
    row_spec = lambda cols: pl.BlockSpec((ROW_TILE, cols), lambda i: (i, 0))
    return pl.pallas_call(
        _inproj_kernel,
        grid=(t // ROW_TILE,),
        in_specs=[
            row_spec(d),
            _resident((1, d)),
            _resident((d, q_cols)),
            _resident((q_cols, d)),
            _resident((d, v_cols)),
            _resident((d, v_cols)),
            pl.BlockSpec((ROW_TILE, half), lambda i: (i % tiles_per_seq, 0)),
            pl.BlockSpec((ROW_TILE, half), lambda i: (i % tiles_per_seq, 0)),
            pl.BlockSpec((half, ROW_TILE), lambda i: (0, i % tiles_per_seq)),
            pl.BlockSpec((half, ROW_TILE), lambda i: (0, i % tiles_per_seq)),
        ],
        out_specs=[
            row_spec(q_cols),
            pl.BlockSpec((1, ROW_TILE // SEQ_BLOCK, q_cols, SEQ_BLOCK),
                         lambda i: (i // tiles_per_seq, i % tiles_per_seq, 0, 0)),
            row_spec(v_cols),
            row_spec(v_cols),
        ],
        out_shape=[
            jax.ShapeDtypeStruct((t, q_cols), BF16),
            jax.ShapeDtypeStruct((batch, seq // SEQ_BLOCK, q_cols, SEQ_BLOCK), BF16),
            jax.ShapeDtypeStruct((t, v_cols), BF16),
            jax.ShapeDtypeStruct((t, v_cols), BF16),
        ],
        compiler_params=_params("parallel"),
        name="inproj",
    )(xf, norm_g.reshape(1, d), wq, wkT, wv, wg, cos, sin, cos.T, sin.T)


def _retention_kernel(q_ref, kT_ref, v_ref, gate_ref, x_ref,
                      qdec_ref, kdec_ref, intra_ref, sdec_ref, gn_ref, wo_ref,
                      o_ref, state_ref, y_ref):
    @pl.when(pl.program_id(1) == 0)
    def _():
        state_ref[...] = jnp.zeros_like(state_ref)

    for hd in range(RET_HEADS):
        ql, vl = hd * RET_QK_DIM, hd * RET_V_DIM
        qh = q_ref[:, ql:ql + RET_QK_DIM]
        kTh = kT_ref[0, 0, ql:ql + RET_QK_DIM, :]
        vh = v_ref[:, vl:vl + RET_V_DIM]
        state = state_ref[hd]
        scores = _dot(qh, kTh)
        qd = (qh.astype(F32) * qdec_ref[:, ql:ql + RET_QK_DIM]).astype(BF16)
        cross = _dot(qd, state.astype(BF16))
        kd = (kTh.astype(F32) * kdec_ref[ql:ql + RET_QK_DIM, :]).astype(BF16)
        state_ref[hd] = state * sdec_ref[hd] + _dot(kd, vh)
        o = _dot((scores * intra_ref[hd]).astype(BF16), vh) + cross
        on = _rms_scale(o) * gn_ref[:, vl:vl + RET_V_DIM]
        y_ref[:, vl:vl + RET_V_DIM] = (gate_ref[:, vl:vl + RET_V_DIM].astype(F32) * on).astype(BF16)
    o_ref[...] = x_ref[...] + _dot(y_ref[...], wo_ref[...])


def _retention_tables():
    lg = jnp.log(1.0 - 2.0 ** (-5.0 - jnp.arange(RET_HEADS, dtype=F32)))
    pos = jnp.arange(SEQ_BLOCK, dtype=F32)
    chunk_of = jnp.arange(SEQ_BLOCK) // CHUNK
    visible = chunk_of[None, :] <= chunk_of[:, None]
    intra = jnp.exp(lg[:, None, None] * jnp.abs(pos[:, None] - pos[None, :]))
    intra = jnp.where(visible[None], intra, 0.0)
    q_dec = jnp.exp(lg[:, None] * (pos[None, :] + 1.0))
    k_dec = jnp.exp(lg[:, None] * (SEQ_BLOCK - 1.0 - pos[None, :]))
    s_dec = jnp.exp(lg * SEQ_BLOCK)
    qdec_full = jnp.repeat(q_dec.T, RET_QK_DIM, axis=1)
    kdec_full = jnp.repeat(k_dec, RET_QK_DIM, axis=0)
    sdec_full = jnp.broadcast_to(s_dec[:, None, None], (RET_HEADS, 1, RET_V_DIM))
    return qdec_full, kdec_full, intra, sdec_full


def _retention(q, kT, v, gate, xf, gn_g, w_o, batch, seq):
    t, d = xf.shape
    q_cols = RET_HEADS * RET_QK_DIM
    v_cols = RET_HEADS * RET_V_DIM
    nblk = seq // SEQ_BLOCK
    qdec, kdec, intra, sdec = _retention_tables()
    row_spec = lambda cols: pl.BlockSpec((SEQ_BLOCK, cols), lambda b, j: (b * nblk + j, 0))
    return pl.pallas_call(
        _retention_kernel,
        grid=(batch, nblk),
        in_specs=[
            row_spec(q_cols),
            pl.BlockSpec((1, 1, q_cols, SEQ_BLOCK), lambda b, j: (b, j, 0, 0)),
            row_spec(v_cols),
            row_spec(v_cols),
            row_spec(d),
            _resident((SEQ_BLOCK, q_cols)),
            _resident((q_cols, SEQ_BLOCK)),
            _resident((RET_HEADS, SEQ_BLOCK, SEQ_BLOCK)),
            _resident((RET_HEADS, 1, RET_V_DIM)),
            _resident((1, v_cols)),
            _resident((v_cols, d)),
        ],
        out_specs=row_spec(d),
        out_shape=jax.ShapeDtypeStruct((t, d), F32),
        scratch_shapes=[
            pltpu.VMEM((RET_HEADS, RET_QK_DIM, RET_V_DIM), F32),
            pltpu.VMEM((SEQ_BLOCK, v_cols), BF16),
        ],
        compiler_params=_params("parallel", "arbitrary"),
        name="retention",
    )(q, kT, v, gate, xf, qdec, kdec, intra, sdec, gn_g.reshape(1, v_cols), w_o.astype(BF16))


def _ffn_kernel(x_ref, g_ref, wgu_ref, wd_ref, o_ref, act_ref):
    x = x_ref[...]
    h = (_rms_scale(x) * g_ref[...]).astype(BF16)
    hidden = wd_ref.shape[0]
    for c in range(0, hidden, FFN_CHUNK):
        gate = _dot(h, wgu_ref[:, c:c + FFN_CHUNK])
        up = _dot(h, wgu_ref[:, hidden + c:hidden + c + FFN_CHUNK])
        act_ref[:, c:c + FFN_CHUNK] = (_silu(gate) * up).astype(BF16)
    o_ref[...] = x + _dot(act_ref[...], wd_ref[...])


def _ffn(xf, norm_g, w_gu, w_down):
    t, d = xf.shape
    hidden = w_down.shape[0]
    row_spec = pl.BlockSpec((FFN_ROW_TILE, d), lambda i: (i, 0))
    return pl.pallas_call(
        _ffn_kernel,
        grid=(t // FFN_ROW_TILE,),
        in_specs=[row_spec, _resident((1, d)), _resident((d, 2 * hidden)), _resident((hidden, d))],
        out_specs=row_spec,
        out_shape=jax.ShapeDtypeStruct((t, d), F32),
        scratch_shapes=[pltpu.VMEM((FFN_ROW_TILE, hidden), BF16)],
        compiler_params=_params("parallel"),
        name="ffn",
    )(xf, norm_g.reshape(1, d), w_gu.astype(BF16), w_down.astype(BF16))


def _kvq_kernel(x_ref, gkv_ref, gq_ref, wkT_ref, wv_ref, wq_ref, kg_ref,
                q_ref, kT_ref, v_ref):
    xn = _rms_scale(x_ref[...])
    u = (xn * gkv_ref[...]).astype(BF16)
    hq = (xn * gq_ref[...]).astype(BF16)
    d = x_ref.shape[1]
    first_head = lax.broadcasted_iota(jnp.int32, (x_ref.shape[0], LANE), 1) < ATT_HEAD_DIM
    for c in range(0, d, 512):
        q_ref[:, c:c + 512] = _dot(hq, wq_ref[:, c:c + 512]).astype(BF16)
        vv = _dot(u, wv_ref[:, c:c + 512])
        for i in range(0, 512, LANE):
            pair_v = vv[:, i:i + LANE]
            v_ref[:, 2 * (c + i):2 * (c + i) + LANE] = jnp.where(first_head, pair_v, 1.0).astype(BF16)
            v_ref[:, 2 * (c + i) + LANE:2 * (c + i + LANE)] = jnp.where(first_head, 1.0, pair_v).astype(BF16)
    kg = kg_ref[...]
    for c in range(0, d, 256):
        kT = _dot_nt(wkT_ref[c:c + 256, :], u)
        for j in range(0, 256, ATT_HEAD_DIM):
            kh = kT[j:j + ATT_HEAD_DIM]
            inv = lax.rsqrt(jnp.mean(kh * kh, axis=0, keepdims=True) + EPS)
            _store_seq_blocks(kT_ref, c + j, (kh * inv * kg).astype(BF16))


def _kvq(xf, kv_norm_g, w_kv, k_norm_g, q_in_norm_g, w_q, batch, seq):
    t, d = xf.shape
    tiles_per_seq = seq // ROW_TILE
    row_spec = pl.BlockSpec((ROW_TILE, d), lambda i: (i, 0))
    return pl.pallas_call(
        _kvq_kernel,
        grid=(t // ROW_TILE,),
        in_specs=[
            row_spec, _resident((1, d)), _resident((1, d)),
            _resident((d, d)), _resident((d, d)), _resident((d, d)),
            _resident((ATT_HEAD_DIM, 1)),
        ],
        out_specs=[
            row_spec,
            pl.BlockSpec((1, ROW_TILE // SEQ_BLOCK, d, SEQ_BLOCK),
                         lambda i: (i // tiles_per_seq, i % tiles_per_seq, 0, 0)),
            pl.BlockSpec((ROW_TILE, 2 * d), lambda i: (i, 0)),
        ],
        out_shape=[
            jax.ShapeDtypeStruct((t, d), BF16),
            jax.ShapeDtypeStruct((batch, seq // SEQ_BLOCK, d, SEQ_BLOCK), BF16),
            jax.ShapeDtypeStruct((t, 2 * d), BF16),
        ],
        compiler_params=_params("parallel"),
        name="kvq",
    )(xf, kv_norm_g.reshape(1, d), q_in_norm_g.reshape(1, d),
      w_kv[:, :d].T.astype(BF16), w_kv[:, d:].astype(BF16), w_q.astype(BF16),
      k_norm_g.reshape(ATT_HEAD_DIM, 1))


def _bias_table_kernel(rb_ref, tab_ref):
    lane = lax.broadcasted_iota(jnp.int32, (REL_PAD, BIAS_VEC), 1)
    entry = lax.broadcasted_iota(jnp.int32, (REL_PAD, BIAS_VEC), 0)
    key_minus_query = jnp.where(lane < PAIR_KEYS, lane, lane - BIAS_VEC)
    dist = PAST_CHUNKS * CHUNK - key_minus_query
    idx = jnp.clip(dist, -REL_CLIP, REL_CLIP) + REL_CLIP
    onehot = (entry == idx).astype(BF16)
    rb = rb_ref[...]
    hi = rb.astype(BF16)
    rest = rb - hi.astype(F32)
    mid = rest.astype(BF16)
    low = (rest - mid.astype(F32)).astype(BF16)
    vec = (_dot(hi, onehot) + _dot(mid, onehot) + _dot(low, onehot)) * LOG2_E
    col = lax.broadcasted_iota(jnp.int32, (CHUNK, PAIR_KEYS), 1)
    for e in range(2):
        in_band = (col >= e * CHUNK) & (col < e * CHUNK + BAND)
        for h in range(ATT_HEADS):
            rows = jnp.broadcast_to(vec[h:h + 1, :], (CHUNK, BIAS_VEC))
            toeplitz = pltpu.roll(rows, e * CHUNK, 1, stride=1, stride_axis=0)
            tab_ref[h, e] = jnp.where(in_band, toeplitz[:, :PAIR_KEYS], -jnp.inf)


def _bias_table(rel_bias):
    heads, table = rel_bias.shape
    rb = jnp.pad(rel_bias, ((0, 0), (0, REL_PAD - table)))
    return pl.pallas_call(
        _bias_table_kernel,
        out_shape=jax.ShapeDtypeStruct((heads, 2, CHUNK, PAIR_KEYS), F32),
        compiler_params=pltpu.CompilerParams(vmem_limit_bytes=VMEM_LIMIT_BYTES),
        name="bias_table",
    )(rb)


_PAIR_PIECES = (
    ((0, 0, SEQ_BLOCK), (1, 0, SEQ_BLOCK), (2, 0, LANE)),
    ((0, LANE, LANE), (1, 0, SEQ_BLOCK), (2, 0, SEQ_BLOCK)),
)


SOFTMAX_LAG = 1
VALUE_LAG = 2
ATT_SLOTS = VALUE_LAG + 1


def _attention_kernel(q_ref, k_ref, v_ref, x_ref, gq_ref, tab_ref, wo_ref, o_ref,
                      kwin_ref, vwin_ref, qm_ref, s_ref, mx_ref, p_ref, ctx_ref):
    j = pl.program_id(1)
    pair_rows = 2 * CHUNK
    copy_rows = 128

    def ring_slot(block):
        return lax.rem(block + WINDOW_BLOCKS, WINDOW_BLOCKS)

    @pl.when(j == 0)
    def _():
        for back in range(1, WINDOW_BLOCKS):
            kwin_ref[ring_slot(j - back)] = jnp.zeros(kwin_ref.shape[1:], BF16)
            vwin_ref[ring_slot(j - back)] = jnp.zeros(vwin_ref.shape[1:], BF16)

    slots = [ring_slot(j + p + 1) for p in range(WINDOW_BLOCKS - 1)]
    k_refs = [kwin_ref.at[slot] for slot in slots] + [k_ref.at[0, 0]]
    v_refs = [vwin_ref.at[slot] for slot in slots] + [v_ref]
    units = [(hp, pair, side) for hp in range(ATT_HEADS // 2)
             for pair in range(len(_PAIR_PIECES)) for side in range(2)]

    def first_head(rows):
        return lax.broadcasted_iota(jnp.int32, (rows, LANE), 1) < ATT_HEAD_DIM

    def scores(u, block_mask):
        hp, pair, side = units[u]
        lo = hp * LANE
        if pair == 0 and side == 0:
            low = first_head(SEQ_BLOCK)
            qp = q_ref[:, lo:lo + LANE].astype(F32)
            sq = qp * qp
            ms0 = jnp.sum(jnp.where(low, sq, 0.0), axis=-1, keepdims=True) / ATT_HEAD_DIM
            ms1 = jnp.sum(jnp.where(low, 0.0, sq), axis=-1, keepdims=True) / ATT_HEAD_DIM
            inv = jnp.where(low, lax.rsqrt(ms0 + EPS), lax.rsqrt(ms1 + EPS))
            qn = qp * inv * gq_ref[:, lo:lo + LANE]
            qm_ref[2 * hp] = jnp.where(low, qn, 0.0).astype(BF16)
            qm_ref[2 * hp + 1] = jnp.where(low, 0.0, qn).astype(BF16)
        qrows = qm_ref[2 * hp + side, pair * pair_rows:(pair + 1) * pair_rows, :]
        slot = u % ATT_SLOTS
        col = 0
        running_max = None
        for blk, c0, width in _PAIR_PIECES[pair]:
            sp = _dot(qrows, k_refs[blk][lo:lo + LANE, c0:c0 + width])
            sp = sp + tab_ref[2 * hp + side, :, col:col + width]
            if block_mask is not None and blk < WINDOW_BLOCKS - 1:
                sp = sp + block_mask[blk]
            s_ref[slot, :, col:col + width] = sp
            for i in range(0, width, LANE):
                tile = sp[:, i:i + LANE]
                running_max = tile if running_max is None else jnp.maximum(running_max, tile)
            col += width
        mx_ref[slot] = running_max

    def softmax(u):
        slot = u % ATT_SLOTS
        for e in range(2):
            rows = slice(e * CHUNK, (e + 1) * CHUNK)
            m = jnp.max(mx_ref[slot, rows, :], axis=-1, keepdims=True)
            for i in range(0, PAIR_KEYS, LANE):
                p_ref[slot, rows, i:i + LANE] = jnp.exp2((s_ref[slot, rows, i:i + LANE] - m).astype(BF16))

    def values(u):
        hp, pair, side = units[u]
        slot = u % ATT_SLOTS
        rows = slice(pair * pair_rows, (pair + 1) * pair_rows)
        lo = (2 * hp + side) * LANE
        acc = None
        col = 0
        for blk, c0, width in _PAIR_PIECES[pair]:
            part = _dot(p_ref[slot, :, col:col + width], v_refs[blk][c0:c0 + width, lo:lo + LANE])
            acc = part if acc is None else acc + part
            col += width
        ctx = acc / pltpu.roll(acc, ATT_HEAD_DIM, 1)
        out_lo = hp * LANE
        if side == 0:
            ctx_ref[rows, out_lo:out_lo + LANE] = ctx.astype(BF16)
        else:
            kept = ctx_ref[rows, out_lo:out_lo + LANE].astype(F32)
            ctx_ref[rows, out_lo:out_lo + LANE] = jnp.where(
                first_head(pair_rows), kept, ctx).astype(BF16)

    def all_units(block_mask):
        for step in range(len(units) + VALUE_LAG):
            if step < len(units):
                scores(step, block_mask)
            if 0 <= step - SOFTMAX_LAG < len(units):
                softmax(step - SOFTMAX_LAG)
            if step >= VALUE_LAG:
                values(step - VALUE_LAG)

    @pl.when(j >= WINDOW_BLOCKS - 1)
    def _():
        all_units(None)

    @pl.when(j < WINDOW_BLOCKS - 1)
    def _():
        all_units([jnp.where(j + p >= WINDOW_BLOCKS - 1, 0.0, -jnp.inf).astype(F32)
                   for p in range(WINDOW_BLOCKS - 1)])

    o_ref[...] = x_ref[...] + _dot(ctx_ref[...], wo_ref[...])
    newest = ring_slot(j)
    for r in range(0, k_ref.shape[2], copy_rows):
        kwin_ref[newest, r:r + copy_rows, :] = k_ref[0, 0, r:r + copy_rows, :]
    for r in range(0, v_ref.shape[0], copy_rows // 2):
        vwin_ref[newest, r:r + copy_rows // 2, :] = v_ref[r:r + copy_rows // 2, :]


def _attention(q, kT, v, xf, q_norm_g, table, w_o, batch, seq):
    t, d = xf.shape
    nblk = seq // SEQ_BLOCK
    gq = jnp.tile(q_norm_g * (ATT_HEAD_DIM ** -0.5 * LOG2_E), ATT_HEADS).reshape(1, d)
    row_spec = lambda cols: pl.BlockSpec((SEQ_BLOCK, cols), lambda b, j: (b * nblk + j, 0))
    return pl.pallas_call(
        _attention_kernel,
        grid=(batch, nblk),
        in_specs=[row_spec(d),
                  pl.BlockSpec((1, 1, d, SEQ_BLOCK), lambda b, j: (b, j, 0, 0)),
                  row_spec(2 * d),
                  row_spec(d), _resident((1, d)),
                  _resident((ATT_HEADS, 2 * CHUNK, PAIR_KEYS)), _resident((d, d))],
        out_specs=row_spec(d),
        out_shape=jax.ShapeDtypeStruct((t, d), F32),
        scratch_shapes=[pltpu.VMEM((WINDOW_BLOCKS, d, SEQ_BLOCK), BF16),
                        pltpu.VMEM((WINDOW_BLOCKS, SEQ_BLOCK, 2 * d), BF16),
                        pltpu.VMEM((ATT_HEADS, SEQ_BLOCK, LANE), BF16),
                        pltpu.VMEM((ATT_SLOTS, 2 * CHUNK, PAIR_KEYS), F32),
                        pltpu.VMEM((ATT_SLOTS, 2 * CHUNK, LANE), F32),
                        pltpu.VMEM((ATT_SLOTS, 2 * CHUNK, PAIR_KEYS), BF16),
                        pltpu.VMEM((SEQ_BLOCK, d), BF16)],
        compiler_params=_params("arbitrary", "arbitrary"),
        name="attention",
    )(q, kT, v, xf, gq, table.reshape(ATT_HEADS, 2 * CHUNK, PAIR_KEYS), w_o.astype(BF16))


def kernel(x, a_norm_g, a_w_in, a_gn_g, a_w_o, a_ffn_norm_g, a_w_gu, a_w_down,
           kv_norm_g, w_kv, k_norm_g,
           b_norm_g, b_w_q, b_q_norm_g, b_rel_bias, b_w_o, b_ffn_norm_g, b_w_gu, b_w_down):
    batch, seq, d = x.shape
    assert seq % ROW_TILE == 0 and seq % SEQ_BLOCK == 0
    assert WINDOW_BLOCKS * CHUNKS_PER_BLOCK >= PAST_CHUNKS + CHUNKS_PER_BLOCK
    assert b_norm_g.shape[0] == 1, "the query projection is fused with the shared K/V projection"
    xf = x.reshape(batch * seq, d)
    for i in range(a_norm_g.shape[0]):
        q, kT, v, gate = _inproj(xf, a_norm_g[i], a_w_in[i], batch, seq)
        xf = _retention(q, kT, v, gate, xf, a_gn_g[i], a_w_o[i], batch, seq)
        xf = _ffn(xf, a_ffn_norm_g[i], a_w_gu[i], a_w_down[i])
    q, kT, v = _kvq(xf, kv_norm_g, w_kv, k_norm_g, b_norm_g[0], b_w_q[0], batch, seq)
    table = _bias_table(b_rel_bias[0])
    xf = _attention(q, kT, v, xf, b_q_norm_g[0], table, b_w_o[0], batch, seq)
    xf = _ffn(xf, b_ffn_norm_g[0], b_w_gu[0], b_w_down[0])
    return xf.reshape(batch, seq, d)
```

```python
import jax
import jax.numpy as jnp
from jax import lax
from jax.experimental import pallas as pl
from jax.experimental.pallas import tpu as pltpu

F32 = jnp.float32
BF16 = jnp.bfloat16

CHUNK = 64
EPS = 1e-6
RET_HEADS = 4
RET_QK_DIM = 256
RET_V_DIM = 512
ROPE_BASE = 10000.0
ATT_HEADS = 16
ATT_HEAD_DIM = 64
PAST_CHUNKS = 8
REL_CLIP = 256
LOG2_E = 1.4426950408889634

ROW_TILE = 512
FFN_ROW_TILE = 1024
SEQ_BLOCK = 256
CHUNKS_PER_BLOCK = SEQ_BLOCK // CHUNK
WINDOW_BLOCKS = 1 + PAST_CHUNKS // CHUNKS_PER_BLOCK
WINDOW = WINDOW_BLOCKS * SEQ_BLOCK
LANE = 128
SUBLANES = 8
BIAS_VEC = 1024
REL_PAD = 640
FFN_CHUNK = 256
VMEM_LIMIT_BYTES = 56 * 1024 * 1024


def _resident(shape):
    zeros = (0,) * len(shape)
    return pl.BlockSpec(shape, lambda *_: zeros, pipeline_mode=pl.Buffered(1))


def _params(*semantics):
    return pltpu.CompilerParams(dimension_semantics=semantics,
                                vmem_limit_bytes=VMEM_LIMIT_BYTES)


def _rms_scale(x):
    return x * lax.rsqrt(jnp.mean(x * x, axis=-1, keepdims=True) + EPS)


def _dot(a, b):
    return jnp.dot(a, b, preferred_element_type=F32)


def _dot_nt(a, b):
    return lax.dot_general(a, b, (((1,), (1,)), ((), ())), preferred_element_type=F32)


def _silu(x):
    return x * jax.nn.sigmoid(x)


def _store_seq_blocks(ref, row_lo, value):
    for blk in range(ROW_TILE // SEQ_BLOCK):
        ref[0, blk, row_lo:row_lo + value.shape[0], :] = value[:, blk * SEQ_BLOCK:(blk + 1) * SEQ_BLOCK]


def _inproj_kernel(x_ref, g_ref, wq_ref, wkT_ref, wv_ref, wg_ref,
                   cos_ref, sin_ref, cosT_ref, sinT_ref,
                   q_ref, kT_ref, v_ref, gate_ref):
    h = (_rms_scale(x_ref[...]) * g_ref[...]).astype(BF16)
    cos, sin = cos_ref[...], sin_ref[...]
    cosT, sinT = cosT_ref[...], sinT_ref[...]
    half = RET_QK_DIM // 2
    k_scale = RET_QK_DIM ** -0.5
    for hd in range(RET_HEADS):
        lo = hd * RET_QK_DIM
        qh = _dot(h, wq_ref[:, lo:lo + RET_QK_DIM])
        x1, x2 = qh[:, :half], qh[:, half:]
        q_ref[:, lo:lo + half] = (x1 * cos - x2 * sin).astype(BF16)
        q_ref[:, lo + half:lo + RET_QK_DIM] = (x1 * sin + x2 * cos).astype(BF16)
        kh = _dot_nt(wkT_ref[lo:lo + RET_QK_DIM, :], h)
        y1, y2 = kh[:half], kh[half:]
        _store_seq_blocks(kT_ref, lo, ((y1 * cosT - y2 * sinT) * k_scale).astype(BF16))
        _store_seq_blocks(kT_ref, lo + half, ((y1 * sinT + y2 * cosT) * k_scale).astype(BF16))
    for hd in range(RET_HEADS):
        lo = hd * RET_V_DIM
        v_ref[:, lo:lo + RET_V_DIM] = _dot(h, wv_ref[:, lo:lo + RET_V_DIM]).astype(BF16)
        gate_ref[:, lo:lo + RET_V_DIM] = _silu(_dot(h, wg_ref[:, lo:lo + RET_V_DIM])).astype(BF16)


def _inproj(xf, norm_g, w_in, batch, seq):
    t, d = xf.shape
    q_cols = RET_HEADS * RET_QK_DIM
    v_cols = RET_HEADS * RET_V_DIM
    half = RET_QK_DIM // 2
    wq = w_in[:, :q_cols].astype(BF16)
    wkT = w_in[:, q_cols:2 * q_cols].T.astype(BF16)
    wv = w_in[:, 2 * q_cols:2 * q_cols + v_cols].astype(BF16)
    wg = w_in[:, 2 * q_cols + v_cols:].astype(BF16)
    inv_freq = ROPE_BASE ** (-jnp.arange(half, dtype=F32) / half)
    ang = jnp.arange(seq).astype(F32)[:, None] * inv_freq[None, :]
    cos, sin = jnp.cos(ang), jnp.sin(ang)
    tiles_per_seq = seq // ROW_TILE
    row_spec = lambda cols: pl.BlockSpec((ROW_TILE, cols), lambda i: (i, 0))
    return pl.pallas_call(
        _inproj_kernel,
        grid=(t // ROW_TILE,),
        in_specs=[
            row_spec(d),
            _resident((1, d)),
            _resident((d, q_cols)),
            _resident((q_cols, d)),
            _resident((d, v_cols)),
            _resident((d, v_cols)),
            pl.BlockSpec((ROW_TILE, half), lambda i: (i % tiles_per_seq, 0)),
            pl.BlockSpec((ROW_TILE, half), lambda i: (i % tiles_per_seq, 0)),
            pl.BlockSpec((half, ROW_TILE), lambda i: (0, i % tiles_per_seq)),
            pl.BlockSpec((half, ROW_TILE), lambda i: (0, i % tiles_per_seq)),
        ],
        out_specs=[
            row_spec(q_cols),
            pl.BlockSpec((1, ROW_TILE // SEQ_BLOCK, q_cols, SEQ_BLOCK),
                         lambda i: (i // tiles_per_seq, i % tiles_per_seq, 0, 0)),
            row_spec(v_cols),
            row_spec(v_cols),
        ],
        out_shape=[
            jax.ShapeDtypeStruct((t, q_cols), BF16),
            jax.ShapeDtypeStruct((batch, seq // SEQ_BLOCK, q_cols, SEQ_BLOCK), BF16),
            jax.ShapeDtypeStruct((t, v_cols), BF16),
            jax.ShapeDtypeStruct((t, v_cols), BF16),
        ],
        compiler_params=_params("parallel"),
        name="inproj",
    )(xf, norm_g.reshape(1, d), wq, wkT, wv, wg, cos, sin, cos.T, sin.T)


def _retention_kernel(q_ref, kT_ref, v_ref, gate_ref, x_ref,
                      qdec_ref, kdec_ref, intra_ref, sdec_ref, gn_ref, wo_ref,
                      o_ref, state_ref, y_ref):
    @pl.when(pl.program_id(1) == 0)
    def _():
        state_ref[...] = jnp.zeros_like(state_ref)

    for hd in range(RET_HEADS):
        ql, vl = hd * RET_QK_DIM, hd * RET_V_DIM
        qh = q_ref[:, ql:ql + RET_QK_DIM]
        kTh = kT_ref[0, 0, ql:ql + RET_QK_DIM, :]
        vh = v_ref[:, vl:vl + RET_V_DIM]
        state = state_ref[hd]
        scores = _dot(qh, kTh)
        qd = (qh.astype(F32) * qdec_ref[:, ql:ql + RET_QK_DIM]).astype(BF16)
        cross = _dot(qd, state.astype(BF16))
        kd = (kTh.astype(F32) * kdec_ref[ql:ql + RET_QK_DIM, :]).astype(BF16)
        state_ref[hd] = state * sdec_ref[hd] + _dot(kd, vh)
        o = _dot((scores * intra_ref[hd]).astype(BF16), vh) + cross
        on = _rms_scale(o) * gn_ref[:, vl:vl + RET_V_DIM]
        y_ref[:, vl:vl + RET_V_DIM] = (gate_ref[:, vl:vl + RET_V_DIM].astype(F32) * on).astype(BF16)
    o_ref[...] = x_ref[...] + _dot(y_ref[...], wo_ref[...])


def _retention_tables():
    lg = jnp.log(1.0 - 2.0 ** (-5.0 - jnp.arange(RET_HEADS, dtype=F32)))
    pos = jnp.arange(SEQ_BLOCK, dtype=F32)
    chunk_of = jnp.arange(SEQ_BLOCK) // CHUNK
    visible = chunk_of[None, :] <= chunk_of[:, None]
    intra = jnp.exp(lg[:, None, None] * jnp.abs(pos[:, None] - pos[None, :]))
    intra = jnp.where(visible[None], intra, 0.0)
    q_dec = jnp.exp(lg[:, None] * (pos[None, :] + 1.0))
    k_dec = jnp.exp(lg[:, None] * (SEQ_BLOCK - 1.0 - pos[None, :]))
    s_dec = jnp.exp(lg * SEQ_BLOCK)
    qdec_full = jnp.repeat(q_dec.T, RET_QK_DIM, axis=1)
    kdec_full = jnp.repeat(k_dec, RET_QK_DIM, axis=0)
    sdec_full = jnp.broadcast_to(s_dec[:, None, None], (RET_HEADS, 1, RET_V_DIM))
    return qdec_full, kdec_full, intra, sdec_full


def _retention(q, kT, v, gate, xf, gn_g, w_o, batch, seq):
    t, d = xf.shape
    q_cols = RET_HEADS * RET_QK_DIM
    v_cols = RET_HEADS * RET_V_DIM
    nblk = seq // SEQ_BLOCK
    qdec, kdec, intra, sdec = _retention_tables()
    row_spec = lambda cols: pl.BlockSpec((SEQ_BLOCK, cols), lambda b, j: (b * nblk + j, 0))
    return pl.pallas_call(
        _retention_kernel,
        grid=(batch, nblk),
        in_specs=[
            row_spec(q_cols),
            pl.BlockSpec((1, 1, q_cols, SEQ_BLOCK), lambda b, j: (b, j, 0, 0)),
            row_spec(v_cols),
            row_spec(v_cols),
            row_spec(d),
            _resident((SEQ_BLOCK, q_cols)),
            _resident((q_cols, SEQ_BLOCK)),
            _resident((RET_HEADS, SEQ_BLOCK, SEQ_BLOCK)),
            _resident((RET_HEADS, 1, RET_V_DIM)),
            _resident((1, v_cols)),
            _resident((v_cols, d)),
        ],
        out_specs=row_spec(d),
        out_shape=jax.ShapeDtypeStruct((t, d), F32),
        scratch_shapes=[
            pltpu.VMEM((RET_HEADS, RET_QK_DIM, RET_V_DIM), F32),
            pltpu.VMEM((SEQ_BLOCK, v_cols), BF16),
        ],
        compiler_params=_params("parallel", "arbitrary"),
        name="retention",
    )(q, kT, v, gate, xf, qdec, kdec, intra, sdec, gn_g.reshape(1, v_cols), w_o.astype(BF16))


def _ffn_kernel(x_ref, g_ref, wgu_ref, wd_ref, o_ref, act_ref):
    x = x_ref[...]
    h = (_rms_scale(x) * g_ref[...]).astype(BF16)
    hidden = wd_ref.shape[0]
    for c in range(0, hidden, FFN_CHUNK):
        gate = _dot(h, wgu_ref[:, c:c + FFN_CHUNK])
        up = _dot(h, wgu_ref[:, hidden + c:hidden + c + FFN_CHUNK])
        act_ref[:, c:c + FFN_CHUNK] = (_silu(gate) * up).astype(BF16)
    o_ref[...] = x + _dot(act_ref[...], wd_ref[...])


def _ffn(xf, norm_g, w_gu, w_down):
    t, d = xf.shape
    hidden = w_down.shape[0]
    row_spec = pl.BlockSpec((FFN_ROW_TILE, d), lambda i: (i, 0))
    return pl.pallas_call(
        _ffn_kernel,
        grid=(t // FFN_ROW_TILE,),
        in_specs=[row_spec, _resident((1, d)), _resident((d, 2 * hidden)), _resident((hidden, d))],
        out_specs=row_spec,
        out_shape=jax.ShapeDtypeStruct((t, d), F32),
        scratch_shapes=[pltpu.VMEM((FFN_ROW_TILE, hidden), BF16)],
        compiler_params=_params("parallel"),
        name="ffn",
    )(xf, norm_g.reshape(1, d), w_gu.astype(BF16), w_down.astype(BF16))


def _kvq_kernel(x_ref, gkv_ref, gq_ref, wkT_ref, wvT_ref, wqT_ref, kg_ref, qg_ref,
                k_ref, vT_ref, qT_ref):
    xn = _rms_scale(x_ref[...])
    u = (xn * gkv_ref[...]).astype(BF16)
    hq = (xn * gq_ref[...]).astype(BF16)
    rows = x_ref.shape[0]
    d = x_ref.shape[1]
    dh = ATT_HEAD_DIM
    kg, qg = kg_ref[...], qg_ref[...]
    ones = jnp.ones((dh, rows), BF16)
    zeros = jnp.zeros((dh, rows), BF16)
    group = 256
    for c in range(0, d, group):
        kT = _dot_nt(wkT_ref[c:c + group, :], u)
        kn = []
        for j in range(0, group, dh):
            kh = kT[j:j + dh]
            kn.append(kh * lax.rsqrt(jnp.mean(kh * kh, axis=0, keepdims=True) + EPS) * kg)
        k_ref[:, c:c + group] = jnp.concatenate(kn, axis=0).T.astype(BF16)
        vT = _dot_nt(wvT_ref[c:c + group, :], u)
        qT = _dot_nt(wqT_ref[c:c + group, :], hq)
        for j in range(0, group, dh):
            head = (c + j) // dh
            lo = 2 * dh * head
            _store_seq_blocks(vT_ref, lo, vT[j:j + dh].astype(BF16))
            _store_seq_blocks(vT_ref, lo + dh, ones)
            qh = qT[j:j + dh]
            qn = (qh * lax.rsqrt(jnp.mean(qh * qh, axis=0, keepdims=True) + EPS) * qg).astype(BF16)
            _store_seq_blocks(qT_ref, lo + dh * (head % 2), qn)
            _store_seq_blocks(qT_ref, lo + dh * (1 - head % 2), zeros)


def _kvq(xf, kv_norm_g, w_kv, k_norm_g, q_in_norm_g, w_q, q_norm_g, batch, seq):
    t, d = xf.shape
    tiles_per_seq = seq // ROW_TILE
    row_spec = pl.BlockSpec((ROW_TILE, d), lambda i: (i, 0))
    blocked = pl.BlockSpec((1, ROW_TILE // SEQ_BLOCK, 2 * d, SEQ_BLOCK),
                           lambda i: (i // tiles_per_seq, i % tiles_per_seq, 0, 0))
    blocked_shape = jax.ShapeDtypeStruct((batch, seq // SEQ_BLOCK, 2 * d, SEQ_BLOCK), BF16)
    return pl.pallas_call(
        _kvq_kernel,
        grid=(t // ROW_TILE,),
        in_specs=[
            row_spec, _resident((1, d)), _resident((1, d)),
            _resident((d, d)), _resident((d, d)), _resident((d, d)),
            _resident((ATT_HEAD_DIM, 1)), _resident((ATT_HEAD_DIM, 1)),
        ],
        out_specs=[row_spec, blocked, blocked],
        out_shape=[jax.ShapeDtypeStruct((t, d), BF16), blocked_shape, blocked_shape],
        compiler_params=_params("parallel"),
        name="kvq",
    )(xf, kv_norm_g.reshape(1, d), q_in_norm_g.reshape(1, d),
      w_kv[:, :d].T.astype(BF16), w_kv[:, d:].T.astype(BF16), w_q.T.astype(BF16),
      k_norm_g.reshape(ATT_HEAD_DIM, 1),
      (q_norm_g * (ATT_HEAD_DIM ** -0.5 * LOG2_E)).reshape(ATT_HEAD_DIM, 1))


def _bias_table_kernel(rb_ref, tab_ref):
    lane = lax.broadcasted_iota(jnp.int32, (REL_PAD, BIAS_VEC), 1)
    entry = lax.broadcasted_iota(jnp.int32, (REL_PAD, BIAS_VEC), 0)
    query_minus_key = jnp.where(lane < SEQ_BLOCK, lane, lane - BIAS_VEC)
    dist = (WINDOW - SEQ_BLOCK) + query_minus_key
    idx = jnp.clip(dist, -REL_CLIP, REL_CLIP) + REL_CLIP
    onehot = (entry == idx).astype(BF16)
    rb = rb_ref[...]
    hi = rb.astype(BF16)
    rest = rb - hi.astype(F32)
    mid = rest.astype(BF16)
    low = (rest - mid.astype(F32)).astype(BF16)
    vec = (_dot(hi, onehot) + _dot(mid, onehot) + _dot(low, onehot)) * LOG2_E
    k_chunk = lax.broadcasted_iota(jnp.int32, (WINDOW, SEQ_BLOCK), 0) // CHUNK
    q_chunk = lax.broadcasted_iota(jnp.int32, (WINDOW, SEQ_BLOCK), 1) // CHUNK
    visible = (k_chunk >= q_chunk) & (k_chunk <= q_chunk + PAST_CHUNKS)
    for h in range(ATT_HEADS):
        rows = jnp.broadcast_to(vec[h:h + 1, :], (WINDOW, BIAS_VEC))
        toeplitz = pltpu.roll(rows, 0, 1, stride=1, stride_axis=0)
        tab_ref[h] = jnp.where(visible, toeplitz[:, :SEQ_BLOCK], -jnp.inf)


def _bias_table(rel_bias):
    heads, table = rel_bias.shape
    rb = jnp.pad(rel_bias, ((0, 0), (0, REL_PAD - table)))
    return pl.pallas_call(
        _bias_table_kernel,
        out_shape=jax.ShapeDtypeStruct((heads, WINDOW, SEQ_BLOCK), F32),
        compiler_params=pltpu.CompilerParams(vmem_limit_bytes=VMEM_LIMIT_BYTES),
        name="bias_table",
    )(rb)


SOFTMAX_LAG = 1
VALUE_LAG = 2
ATT_SLOTS = VALUE_LAG + 1


def _attention_kernel(k_ref, vT_ref, qT_ref, x_ref, tab_ref, wo_ref, o_ref,
                      kwin_ref, vwin_ref, s_ref, mx_ref, p_ref, ctxT_ref):
    j = pl.program_id(1)
    dh = ATT_HEAD_DIM
    copy_rows = 128

    def ring_slot(block):
        return lax.rem(block + WINDOW_BLOCKS, WINDOW_BLOCKS)

    @pl.when(j == 0)
    def _():
        for back in range(1, WINDOW_BLOCKS):
            kwin_ref[ring_slot(j - back)] = jnp.zeros(kwin_ref.shape[1:], BF16)
            vwin_ref[ring_slot(j - back)] = jnp.zeros(vwin_ref.shape[1:], BF16)

    slots = [ring_slot(j + p + 1) for p in range(WINDOW_BLOCKS - 1)]
    k_refs = [kwin_ref.at[slot] for slot in slots] + [k_ref]
    v_refs = [vwin_ref.at[slot] for slot in slots] + [vT_ref.at[0, 0]]

    def scores(h, block_mask):
        slot = h % ATT_SLOTS
        pair_lo = (h // 2) * 2 * dh
        q_pair = qT_ref[0, 0, 2 * dh * h:2 * dh * (h + 1), :]
        running_max = None
        for blk in range(WINDOW_BLOCKS):
            rows = slice(blk * SEQ_BLOCK, (blk + 1) * SEQ_BLOCK)
            sp = _dot(k_refs[blk][:, pair_lo:pair_lo + 2 * dh], q_pair) + tab_ref[h, rows, :]
            if block_mask is not None and blk < WINDOW_BLOCKS - 1:
                sp = sp + block_mask[blk]
            s_ref[slot, rows, :] = sp
            for r in range(0, SEQ_BLOCK, SUBLANES):
                tile = sp[r:r + SUBLANES]
                running_max = tile if running_max is None else jnp.maximum(running_max, tile)
        mx_ref[slot] = running_max

    def softmax(h):
        slot = h % ATT_SLOTS
        m = jnp.max(mx_ref[slot], axis=0, keepdims=True)
        for r in range(0, WINDOW, LANE):
            p_ref[slot, r:r + LANE, :] = jnp.exp2((s_ref[slot, r:r + LANE, :] - m).astype(BF16))

    def values(h):
        slot = h % ATT_SLOTS
        acc = None
        for blk in range(WINDOW_BLOCKS):
            rows = slice(blk * SEQ_BLOCK, (blk + 1) * SEQ_BLOCK)
            part = _dot(v_refs[blk][2 * dh * h:2 * dh * (h + 1), :], p_ref[slot, rows, :])
            acc = part if acc is None else acc + part
        ctxT_ref[dh * h:dh * (h + 1), :] = acc[:dh] / acc[dh:]

    def all_heads(block_mask):
        for step in range(ATT_HEADS + VALUE_LAG):
            if step < ATT_HEADS:
                scores(step, block_mask)
            if 0 <= step - SOFTMAX_LAG < ATT_HEADS:
                softmax(step - SOFTMAX_LAG)
            if step >= VALUE_LAG:
                values(step - VALUE_LAG)

    @pl.when(j >= WINDOW_BLOCKS - 1)
    def _():
        all_heads(None)

    @pl.when(j < WINDOW_BLOCKS - 1)
    def _():
        all_heads([jnp.where(j + p >= WINDOW_BLOCKS - 1, 0.0, -jnp.inf).astype(F32)
                   for p in range(WINDOW_BLOCKS - 1)])

    o_ref[...] = x_ref[...] + _dot(ctxT_ref[...].T.astype(BF16), wo_ref[...])
    newest = ring_slot(j)
    for r in range(0, SEQ_BLOCK, copy_rows):
        kwin_ref[newest, r:r + copy_rows, :] = k_ref[r:r + copy_rows, :]
    for r in range(0, vT_ref.shape[2], 2 * copy_rows):
        vwin_ref[newest, r:r + 2 * copy_rows, :] = vT_ref[0, 0, r:r + 2 * copy_rows, :]


def _attention(k, vT, qT, xf, table, w_o, batch, seq):
    t, d = xf.shape
    nblk = seq // SEQ_BLOCK
    row_spec = pl.BlockSpec((SEQ_BLOCK, d), lambda b, j: (b * nblk + j, 0))
    blocked = pl.BlockSpec((1, 1, 2 * d, SEQ_BLOCK), lambda b, j: (b, j, 0, 0))
    return pl.pallas_call(
        _attention_kernel,
        grid=(batch, nblk),
        in_specs=[row_spec, blocked, blocked, row_spec,
                  _resident((ATT_HEADS, WINDOW, SEQ_BLOCK)), _resident((d, d))],
        out_specs=row_spec,
        out_shape=jax.ShapeDtypeStruct((t, d), F32),
        scratch_shapes=[pltpu.VMEM((WINDOW_BLOCKS, SEQ_BLOCK, d), BF16),
                        pltpu.VMEM((WINDOW_BLOCKS, 2 * d, SEQ_BLOCK), BF16),
                        pltpu.VMEM((ATT_SLOTS, WINDOW, SEQ_BLOCK), F32),
                        pltpu.VMEM((ATT_SLOTS, SUBLANES, SEQ_BLOCK), F32),
                        pltpu.VMEM((ATT_SLOTS, WINDOW, SEQ_BLOCK), BF16),
                        pltpu.VMEM((d, SEQ_BLOCK), F32)],
        compiler_params=_params("arbitrary", "arbitrary"),
        name="attention",
    )(k, vT, qT, xf, table, w_o.astype(BF16))


def kernel(x, a_norm_g, a_w_in, a_gn_g, a_w_o, a_ffn_norm_g, a_w_gu, a_w_down,
           kv_norm_g, w_kv, k_norm_g,
           b_norm_g, b_w_q, b_q_norm_g, b_rel_bias, b_w_o, b_ffn_norm_g, b_w_gu, b_w_down):
    batch, seq, d = x.shape
    assert seq % ROW_TILE == 0 and seq % SEQ_BLOCK == 0 and (batch * seq) % FFN_ROW_TILE == 0
    assert WINDOW_BLOCKS * CHUNKS_PER_BLOCK >= PAST_CHUNKS + CHUNKS_PER_BLOCK
    assert b_norm_g.shape[0] == 1, "the query projection is fused with the shared K/V projection"
    xf = x.reshape(batch * seq, d)
    for i in range(a_norm_g.shape[0]):
        q, kT, v, gate = _inproj(xf, a_norm_g[i], a_w_in[i], batch, seq)
        xf = _retention(q, kT, v, gate, xf, a_gn_g[i], a_w_o[i], batch, seq)
        xf = _ffn(xf, a_ffn_norm_g[i], a_w_gu[i], a_w_down[i])
    k, vT, qT = _kvq(xf, kv_norm_g, w_kv, k_norm_g, b_norm_g[0], b_w_q[0], b_q_norm_g[0], batch, seq)
    table = _bias_table(b_rel_bias[0])
    xf = _attention(k, vT, qT, xf, table, b_w_o[0], batch, seq)
    xf = _ffn(xf, b_ffn_norm_g[0], b_w_gu[0], b_w_down[0])
    return xf.reshape(batch, seq, d)
```

```python
import jax
import jax.numpy as jnp
from jax import lax
from jax.experimental import pallas as pl
from jax.experimental.pallas import tpu as pltpu

F32 = jnp.float32
BF16 = jnp.bfloat16

CHUNK = 64
EPS = 1e-6
RET_HEADS = 4
RET_QK_DIM = 256
RET_V_DIM = 512
ROPE_BASE = 10000.0
ATT_HEADS = 16
ATT_HEAD_DIM = 64
PAST_CHUNKS = 8
REL_CLIP = 256
LOG2_E = 1.4426950408889634

ROW_TILE = 512
KVQ_ROW_TILE = 1024
FFN_ROW_TILE = 1024
SEQ_BLOCK = 256
RET_BLOCKS_PER_STEP = 2
CHUNKS_PER_BLOCK = SEQ_BLOCK // CHUNK
WINDOW_BLOCKS = 1 + PAST_CHUNKS // CHUNKS_PER_BLOCK
WINDOW = WINDOW_BLOCKS * SEQ_BLOCK
LANE = 128
SUBLANES = 8
BIAS_VEC = 1024
REL_PAD = 640
FFN_CHUNK = 256
VMEM_LIMIT_BYTES = 56 * 1024 * 1024


def _resident(shape):
    zeros = (0,) * len(shape)
    return pl.BlockSpec(shape, lambda *_: zeros, pipeline_mode=pl.Buffered(1))


def _params(*semantics):
    return pltpu.CompilerParams(dimension_semantics=semantics,
                                vmem_limit_bytes=VMEM_LIMIT_BYTES)


def _rms_scale(x):
    return x * lax.rsqrt(jnp.mean(x * x, axis=-1, keepdims=True) + EPS)


def _dot(a, b):
    return jnp.dot(a, b, preferred_element_type=F32)


def _dot_nt(a, b):
    return lax.dot_general(a, b, (((1,), (1,)), ((), ())), preferred_element_type=F32)


def _silu(x):
    return x * jax.nn.sigmoid(x)


def _store_seq_blocks(ref, row_lo, value):
    for blk in range(value.shape[1] // SEQ_BLOCK):
        ref[0, blk, row_lo:row_lo + value.shape[0], :] = value[:, blk * SEQ_BLOCK:(blk + 1) * SEQ_BLOCK]


def _inproj_kernel(x_ref, g_ref, wq_ref, wkT_ref, wv_ref, wg_ref,
                   cos_ref, sin_ref, cosT_ref, sinT_ref,
                   q_ref, kT_ref, v_ref, gate_ref):
    h = (_rms_scale(x_ref[...]) * g_ref[...]).astype(BF16)
    cos, sin = cos_ref[...], sin_ref[...]
    cosT, sinT = cosT_ref[...], sinT_ref[...]
    half = RET_QK_DIM // 2
    k_scale = RET_QK_DIM ** -0.5
    for hd in range(RET_HEADS):
        lo = hd * RET_QK_DIM
        qh = _dot(h, wq_ref[:, lo:lo + RET_QK_DIM])
        x1, x2 = qh[:, :half], qh[:, half:]
        q_ref[:, lo:lo + half] = (x1 * cos - x2 * sin).astype(BF16)
        q_ref[:, lo + half:lo + RET_QK_DIM] = (x1 * sin + x2 * cos).astype(BF16)
        kh = _dot_nt(wkT_ref[lo:lo + RET_QK_DIM, :], h)
        y1, y2 = kh[:half], kh[half:]
        _store_seq_blocks(kT_ref, lo, ((y1 * cosT - y2 * sinT) * k_scale).astype(BF16))
        _store_seq_blocks(kT_ref, lo + half, ((y1 * sinT + y2 * cosT) * k_scale).astype(BF16))
    for hd in range(RET_HEADS):
        lo = hd * RET_V_DIM
        v_ref[:, lo:lo + RET_V_DIM] = _dot(h, wv_ref[:, lo:lo + RET_V_DIM]).astype(BF16)
        gate_ref[:, lo:lo + RET_V_DIM] = _silu(_dot(h, wg_ref[:, lo:lo + RET_V_DIM])).astype(BF16)


def _inproj(xf, norm_g, w_in, batch, seq):
    t, d = xf.shape
    q_cols = RET_HEADS * RET_QK_DIM
    v_cols = RET_HEADS * RET_V_DIM
    half = RET_QK_DIM // 2
    wq = w_in[:, :q_cols].astype(BF16)
    wkT = w_in[:, q_cols:2 * q_cols].T.astype(BF16)
    wv = w_in[:, 2 * q_cols:2 * q_cols + v_cols].astype(BF16)
    wg = w_in[:, 2 * q_cols + v_cols:].astype(BF16)
    inv_freq = ROPE_BASE ** (-jnp.arange(half, dtype=F32) / half)
    ang = jnp.arange(seq).astype(F32)[:, None] * inv_freq[None, :]
    cos, sin = jnp.cos(ang), jnp.sin(ang)
    tiles_per_seq = seq // ROW_TILE
    row_spec = lambda cols: pl.BlockSpec((ROW_TILE, cols), lambda i: (i, 0))
    return pl.pallas_call(
        _inproj_kernel,
        grid=(t // ROW_TILE,),
        in_specs=[
            row_spec(d),
            _resident((1, d)),
            _resident((d, q_cols)),
            _resident((q_cols, d)),
            _resident((d, v_cols)),
            _resident((d, v_cols)),
            pl.BlockSpec((ROW_TILE, half), lambda i: (i % tiles_per_seq, 0)),
            pl.BlockSpec((ROW_TILE, half), lambda i: (i % tiles_per_seq, 0)),
            pl.BlockSpec((half, ROW_TILE), lambda i: (0, i % tiles_per_seq)),
            pl.BlockSpec((half, ROW_TILE), lambda i: (0, i % tiles_per_seq)),
        ],
        out_specs=[
            row_spec(q_cols),
            pl.BlockSpec((1, ROW_TILE // SEQ_BLOCK, q_cols, SEQ_BLOCK),
                         lambda i: (i // tiles_per_seq, i % tiles_per_seq, 0, 0)),
            row_spec(v_cols),
            row_spec(v_cols),
        ],
        out_shape=[
            jax.ShapeDtypeStruct((t, q_cols), BF16),
            jax.ShapeDtypeStruct((batch, seq // SEQ_BLOCK, q_cols, SEQ_BLOCK), BF16),
            jax.ShapeDtypeStruct((t, v_cols), BF16),
            jax.ShapeDtypeStruct((t, v_cols), BF16),
        ],
        compiler_params=_params("parallel"),
        name="inproj",
    )(xf, norm_g.reshape(1, d), wq, wkT, wv, wg, cos, sin, cos.T, sin.T)


def _retention_kernel(q_ref, kT_ref, v_ref, gate_ref, x_ref,
                      qdec_ref, kdec_ref, intra_ref, sdec_ref, gn_ref, wo_ref,
                      o_ref, state_ref, y_ref):
    @pl.when(pl.program_id(1) == 0)
    def _():
        state_ref[...] = jnp.zeros_like(state_ref)

    for sub in range(RET_BLOCKS_PER_STEP):
        rows = slice(sub * SEQ_BLOCK, (sub + 1) * SEQ_BLOCK)
        for hd in range(RET_HEADS):
            ql, vl = hd * RET_QK_DIM, hd * RET_V_DIM
            qh = q_ref[rows, ql:ql + RET_QK_DIM]
            kTh = kT_ref[0, sub, ql:ql + RET_QK_DIM, :]
            vh = v_ref[rows, vl:vl + RET_V_DIM]
            state = state_ref[hd]
            scores = _dot(qh, kTh)
            qd = (qh.astype(F32) * qdec_ref[:, ql:ql + RET_QK_DIM]).astype(BF16)
            cross = _dot(qd, state.astype(BF16))
            kd = (kTh.astype(F32) * kdec_ref[ql:ql + RET_QK_DIM, :]).astype(BF16)
            state_ref[hd] = state * sdec_ref[hd] + _dot(kd, vh)
            o = _dot((scores * intra_ref[hd]).astype(BF16), vh) + cross
            on = _rms_scale(o) * gn_ref[:, vl:vl + RET_V_DIM]
            y_ref[rows, vl:vl + RET_V_DIM] = (
                gate_ref[rows, vl:vl + RET_V_DIM].astype(F32) * on).astype(BF16)
    o_ref[...] = x_ref[...] + _dot(y_ref[...], wo_ref[...])


def _retention_tables():
    lg = jnp.log(1.0 - 2.0 ** (-5.0 - jnp.arange(RET_HEADS, dtype=F32)))
    pos = jnp.arange(SEQ_BLOCK, dtype=F32)
    chunk_of = jnp.arange(SEQ_BLOCK) // CHUNK
    visible = chunk_of[None, :] <= chunk_of[:, None]
    intra = jnp.exp(lg[:, None, None] * jnp.abs(pos[:, None] - pos[None, :]))
    intra = jnp.where(visible[None], intra, 0.0)
    q_dec = jnp.exp(lg[:, None] * (pos[None, :] + 1.0))
    k_dec = jnp.exp(lg[:, None] * (SEQ_BLOCK - 1.0 - pos[None, :]))
    s_dec = jnp.exp(lg * SEQ_BLOCK)
    qdec_full = jnp.repeat(q_dec.T, RET_QK_DIM, axis=1)
    kdec_full = jnp.repeat(k_dec, RET_QK_DIM, axis=0)
    sdec_full = jnp.broadcast_to(s_dec[:, None, None], (RET_HEADS, 1, RET_V_DIM))
    return qdec_full, kdec_full, intra, sdec_full


def _retention(q, kT, v, gate, xf, gn_g, w_o, batch, seq):
    t, d = xf.shape
    q_cols = RET_HEADS * RET_QK_DIM
    v_cols = RET_HEADS * RET_V_DIM
    step_rows = RET_BLOCKS_PER_STEP * SEQ_BLOCK
    nstep = seq // step_rows
    qdec, kdec, intra, sdec = _retention_tables()
    row_spec = lambda cols: pl.BlockSpec((step_rows, cols), lambda b, j: (b * nstep + j, 0))
    return pl.pallas_call(
        _retention_kernel,
        grid=(batch, nstep),
        in_specs=[
            row_spec(q_cols),
            pl.BlockSpec((1, RET_BLOCKS_PER_STEP, q_cols, SEQ_BLOCK), lambda b, j: (b, j, 0, 0)),
            row_spec(v_cols),
            row_spec(v_cols),
            row_spec(d),
            _resident((SEQ_BLOCK, q_cols)),
            _resident((q_cols, SEQ_BLOCK)),
            _resident((RET_HEADS, SEQ_BLOCK, SEQ_BLOCK)),
            _resident((RET_HEADS, 1, RET_V_DIM)),
            _resident((1, v_cols)),
            _resident((v_cols, d)),
        ],
        out_specs=row_spec(d),
        out_shape=jax.ShapeDtypeStruct((t, d), F32),
        scratch_shapes=[
            pltpu.VMEM((RET_HEADS, RET_QK_DIM, RET_V_DIM), F32),
            pltpu.VMEM((step_rows, v_cols), BF16),
        ],
        compiler_params=_params("parallel", "arbitrary"),
        name="retention",
    )(q, kT, v, gate, xf, qdec, kdec, intra, sdec, gn_g.reshape(1, v_cols), w_o.astype(BF16))


def _ffn_kernel(x_ref, g_ref, wgu_ref, wd_ref, o_ref, act_ref):
    x = x_ref[...]
    h = (_rms_scale(x) * g_ref[...]).astype(BF16)
    hidden = wd_ref.shape[0]
    for c in range(0, hidden, FFN_CHUNK):
        gate = _dot(h, wgu_ref[:, c:c + FFN_CHUNK])
        up = _dot(h, wgu_ref[:, hidden + c:hidden + c + FFN_CHUNK])
        act_ref[:, c:c + FFN_CHUNK] = (_silu(gate) * up).astype(BF16)
    o_ref[...] = x + _dot(act_ref[...], wd_ref[...])


def _ffn(xf, norm_g, w_gu, w_down):
    t, d = xf.shape
    hidden = w_down.shape[0]
    row_spec = pl.BlockSpec((FFN_ROW_TILE, d), lambda i: (i, 0))
    return pl.pallas_call(
        _ffn_kernel,
        grid=(t // FFN_ROW_TILE,),
        in_specs=[row_spec, _resident((1, d)), _resident((d, 2 * hidden)), _resident((hidden, d))],
        out_specs=row_spec,
        out_shape=jax.ShapeDtypeStruct((t, d), F32),
        scratch_shapes=[pltpu.VMEM((FFN_ROW_TILE, hidden), BF16)],
        compiler_params=_params("parallel"),
        name="ffn",
    )(xf, norm_g.reshape(1, d), w_gu.astype(BF16), w_down.astype(BF16))


def _kvq_kernel(x_ref, gkv_ref, gq_ref, wkT_ref, wvT_ref, wqT_ref, kg_ref, qg_ref,
                k_ref, vT_ref, qT_ref):
    xn = _rms_scale(x_ref[...])
    u = (xn * gkv_ref[...]).astype(BF16)
    hq = (xn * gq_ref[...]).astype(BF16)
    rows = x_ref.shape[0]
    d = x_ref.shape[1]
    dh = ATT_HEAD_DIM
    kg, qg = kg_ref[...], qg_ref[...]
    ones = jnp.ones((dh, rows), BF16)
    zeros = jnp.zeros((dh, rows), BF16)
    group = 256
    for c in range(0, d, group):
        kT = _dot_nt(wkT_ref[c:c + group, :], u)
        kn = []
        for j in range(0, group, dh):
            kh = kT[j:j + dh]
            kn.append(kh * lax.rsqrt(jnp.mean(kh * kh, axis=0, keepdims=True) + EPS) * kg)
        k_ref[:, c:c + group] = jnp.concatenate(kn, axis=0).T.astype(BF16)
        vT = _dot_nt(wvT_ref[c:c + group, :], u)
        qT = _dot_nt(wqT_ref[c:c + group, :], hq)
        for j in range(0, group, dh):
            head = (c + j) // dh
            lo = 2 * dh * head
            _store_seq_blocks(vT_ref, lo, vT[j:j + dh].astype(BF16))
            _store_seq_blocks(vT_ref, lo + dh, ones)
            qh = qT[j:j + dh]
            qn = (qh * lax.rsqrt(jnp.mean(qh * qh, axis=0, keepdims=True) + EPS) * qg).astype(BF16)
            _store_seq_blocks(qT_ref, lo + dh * (head % 2), qn)
            _store_seq_blocks(qT_ref, lo + dh * (1 - head % 2), zeros)


def _kvq(xf, kv_norm_g, w_kv, k_norm_g, q_in_norm_g, w_q, q_norm_g, batch, seq):
    t, d = xf.shape
    tiles_per_seq = seq // KVQ_ROW_TILE
    row_spec = pl.BlockSpec((KVQ_ROW_TILE, d), lambda i: (i, 0))
    blocked = pl.BlockSpec((1, KVQ_ROW_TILE // SEQ_BLOCK, 2 * d, SEQ_BLOCK),
                           lambda i: (i // tiles_per_seq, i % tiles_per_seq, 0, 0))
    blocked_shape = jax.ShapeDtypeStruct((batch, seq // SEQ_BLOCK, 2 * d, SEQ_BLOCK), BF16)
    return pl.pallas_call(
        _kvq_kernel,
        grid=(t // KVQ_ROW_TILE,),
        in_specs=[
            row_spec, _resident((1, d)), _resident((1, d)),
            _resident((d, d)), _resident((d, d)), _resident((d, d)),
            _resident((ATT_HEAD_DIM, 1)), _resident((ATT_HEAD_DIM, 1)),
        ],
        out_specs=[row_spec, blocked, blocked],
        out_shape=[jax.ShapeDtypeStruct((t, d), BF16), blocked_shape, blocked_shape],
        compiler_params=_params("parallel"),
        name="kvq",
    )(xf, kv_norm_g.reshape(1, d), q_in_norm_g.reshape(1, d),
      w_kv[:, :d].T.astype(BF16), w_kv[:, d:].T.astype(BF16), w_q.T.astype(BF16),
      k_norm_g.reshape(ATT_HEAD_DIM, 1),
      (q_norm_g * (ATT_HEAD_DIM ** -0.5 * LOG2_E)).reshape(ATT_HEAD_DIM, 1))


def _bias_table_kernel(rb_ref, tab_ref):
    lane = lax.broadcasted_iota(jnp.int32, (REL_PAD, BIAS_VEC), 1)
    entry = lax.broadcasted_iota(jnp.int32, (REL_PAD, BIAS_VEC), 0)
    query_minus_key = jnp.where(lane < SEQ_BLOCK, lane, lane - BIAS_VEC)
    dist = (WINDOW - SEQ_BLOCK) + query_minus_key
    idx = jnp.clip(dist, -REL_CLIP, REL_CLIP) + REL_CLIP
    onehot = (entry == idx).astype(BF16)
    rb = rb_ref[...]
    hi = rb.astype(BF16)
    rest = rb - hi.astype(F32)
    mid = rest.astype(BF16)
    low = (rest - mid.astype(F32)).astype(BF16)
    vec = (_dot(hi, onehot) + _dot(mid, onehot) + _dot(low, onehot)) * LOG2_E
    k_chunk = lax.broadcasted_iota(jnp.int32, (WINDOW, SEQ_BLOCK), 0) // CHUNK
    q_chunk = lax.broadcasted_iota(jnp.int32, (WINDOW, SEQ_BLOCK), 1) // CHUNK
    visible = (k_chunk >= q_chunk) & (k_chunk <= q_chunk + PAST_CHUNKS)
    for h in range(ATT_HEADS):
        rows = jnp.broadcast_to(vec[h:h + 1, :], (WINDOW, BIAS_VEC))
        toeplitz = pltpu.roll(rows, 0, 1, stride=1, stride_axis=0)
        tab_ref[h] = jnp.where(visible, toeplitz[:, :SEQ_BLOCK], -jnp.inf)


def _bias_table(rel_bias):
    heads, table = rel_bias.shape
    rb = jnp.pad(rel_bias, ((0, 0), (0, REL_PAD - table)))
    return pl.pallas_call(
        _bias_table_kernel,
        out_shape=jax.ShapeDtypeStruct((heads, WINDOW, SEQ_BLOCK), F32),
        compiler_params=pltpu.CompilerParams(vmem_limit_bytes=VMEM_LIMIT_BYTES),
        name="bias_table",
    )(rb)


SOFTMAX_LAG = 1
VALUE_LAG = 2


def _attention_kernel(k_ref, vT_ref, qT_ref, x_ref, tab_ref, wo_ref, o_ref,
                      kwin_ref, vwin_ref, ctxT_ref):
    j = pl.program_id(1)
    dh = ATT_HEAD_DIM
    copy_rows = 128

    def ring_slot(block):
        return lax.rem(block + WINDOW_BLOCKS, WINDOW_BLOCKS)

    @pl.when(j == 0)
    def _():
        for back in range(1, WINDOW_BLOCKS):
            kwin_ref[ring_slot(j - back)] = jnp.zeros(kwin_ref.shape[1:], BF16)
            vwin_ref[ring_slot(j - back)] = jnp.zeros(vwin_ref.shape[1:], BF16)

    slots = [ring_slot(j + p + 1) for p in range(WINDOW_BLOCKS - 1)]
    k_refs = [kwin_ref.at[slot] for slot in slots] + [k_ref]
    v_refs = [vwin_ref.at[slot] for slot in slots] + [vT_ref.at[0, 0]]

    in_flight = {}

    def scores(h, block_mask):
        pair_lo = (h // 2) * 2 * dh
        q_pair = qT_ref[0, 0, 2 * dh * h:2 * dh * (h + 1), :]
        running_max = None
        pieces = []
        for blk in range(WINDOW_BLOCKS):
            rows = slice(blk * SEQ_BLOCK, (blk + 1) * SEQ_BLOCK)
            sp = _dot(k_refs[blk][:, pair_lo:pair_lo + 2 * dh], q_pair) + tab_ref[h, rows, :]
            if block_mask is not None and blk < WINDOW_BLOCKS - 1:
                sp = sp + block_mask[blk]
            pieces.append(sp)
            for r in range(0, SEQ_BLOCK, SUBLANES):
                tile = sp[r:r + SUBLANES]
                running_max = tile if running_max is None else jnp.maximum(running_max, tile)
        in_flight[h] = (pieces, running_max)

    def softmax(h):
        pieces, running_max = in_flight[h]
        m = jnp.max(running_max, axis=0, keepdims=True)
        in_flight[h] = [jnp.exp2(sp - m).astype(BF16) for sp in pieces]

    def values(h):
        acc = None
        for blk, p in enumerate(in_flight.pop(h)):
            part = _dot(v_refs[blk][2 * dh * h:2 * dh * (h + 1), :], p)
            acc = part if acc is None else acc + part
        ctxT_ref[dh * h:dh * (h + 1), :] = acc[:dh] / acc[dh:]

    def all_heads(block_mask):
        in_flight.clear()
        for step in range(ATT_HEADS + VALUE_LAG):
            if step < ATT_HEADS:
                scores(step, block_mask)
            if 0 <= step - SOFTMAX_LAG < ATT_HEADS:
                softmax(step - SOFTMAX_LAG)
            if step >= VALUE_LAG:
                values(step - VALUE_LAG)

    @pl.when(j >= WINDOW_BLOCKS - 1)
    def _():
        all_heads(None)

    @pl.when(j < WINDOW_BLOCKS - 1)
    def _():
        all_heads([jnp.where(j + p >= WINDOW_BLOCKS - 1, 0.0, -jnp.inf).astype(F32)
                   for p in range(WINDOW_BLOCKS - 1)])

    o_ref[...] = x_ref[...] + _dot(ctxT_ref[...].T.astype(BF16), wo_ref[...])
    newest = ring_slot(j)
    for r in range(0, SEQ_BLOCK, copy_rows):
        kwin_ref[newest, r:r + copy_rows, :] = k_ref[r:r + copy_rows, :]
    for r in range(0, vT_ref.shape[2], 2 * copy_rows):
        vwin_ref[newest, r:r + 2 * copy_rows, :] = vT_ref[0, 0, r:r + 2 * copy_rows, :]


def _attention(k, vT, qT, xf, table, w_o, batch, seq):
    t, d = xf.shape
    nblk = seq // SEQ_BLOCK
    row_spec = pl.BlockSpec((SEQ_BLOCK, d), lambda b, j: (b * nblk + j, 0))
    blocked = pl.BlockSpec((1, 1, 2 * d, SEQ_BLOCK), lambda b, j: (b, j, 0, 0))
    return pl.pallas_call(
        _attention_kernel,
        grid=(batch, nblk),
        in_specs=[row_spec, blocked, blocked, row_spec,
                  _resident((ATT_HEADS, WINDOW, SEQ_BLOCK)), _resident((d, d))],
        out_specs=row_spec,
        out_shape=jax.ShapeDtypeStruct((t, d), F32),
        scratch_shapes=[pltpu.VMEM((WINDOW_BLOCKS, SEQ_BLOCK, d), BF16),
                        pltpu.VMEM((WINDOW_BLOCKS, 2 * d, SEQ_BLOCK), BF16),
                        pltpu.VMEM((d, SEQ_BLOCK), F32)],
        compiler_params=_params("arbitrary", "arbitrary"),
        name="attention",
    )(k, vT, qT, xf, table, w_o.astype(BF16))


def kernel(x, a_norm_g, a_w_in, a_gn_g, a_w_o, a_ffn_norm_g, a_w_gu, a_w_down,
           kv_norm_g, w_kv, k_norm_g,
           b_norm_g, b_w_q, b_q_norm_g, b_rel_bias, b_w_o, b_ffn_norm_g, b_w_gu, b_w_down):
    batch, seq, d = x.shape
    assert seq % ROW_TILE == 0 and seq % KVQ_ROW_TILE == 0 and seq % SEQ_BLOCK == 0
    assert (batch * seq) % FFN_ROW_TILE == 0 and seq % (RET_BLOCKS_PER_STEP * SEQ_BLOCK) == 0
    assert WINDOW_BLOCKS * CHUNKS_PER_BLOCK >= PAST_CHUNKS + CHUNKS_PER_BLOCK
    assert b_norm_g.shape[0] == 1, "the query projection is fused with the shared K/V projection"
    xf = x.reshape(batch * seq, d)
    for i in range(a_norm_g.shape[0]):
        q, kT, v, gate = _inproj(xf, a_norm_g[i], a_w_in[i], batch, seq)
        xf = _retention(q, kT, v, gate, xf, a_gn_g[i], a_w_o[i], batch, seq)
        xf = _ffn(xf, a_ffn_norm_g[i], a_w_gu[i], a_w_down[i])
    k, vT, qT = _kvq(xf, kv_norm_g, w_kv, k_norm_g, b_norm_g[0], b_w_q[0], b_q_norm_g[0], batch, seq)
    table = _bias_table(b_rel_bias[0])
    xf = _attention(k, vT, qT, xf, table, b_w_o[0], batch, seq)
    xf = _ffn(xf, b_ffn_norm_g[0], b_w_gu[0], b_w_down[0])
    return xf.reshape(batch, seq, d)
```

```python
import jax
import jax.numpy as jnp
import numpy as np
from jax import lax
from jax.experimental import pallas as pl
from jax.experimental.pallas import tpu as pltpu

F32 = jnp.float32
BF16 = jnp.bfloat16

CHUNK = 64
EPS = 1e-6
RET_HEADS = 4
RET_QK_DIM = 256
RET_V_DIM = 512
ROPE_BASE = 10000.0
ATT_HEADS = 16
ATT_HEAD_DIM = 64
PAST_CHUNKS = 8
REL_CLIP = 256
LOG2_E = 1.4426950408889634

ROW_TILE = 512
KVQ_ROW_TILE = 1024
FFN_ROW_TILE = 1024
SEQ_BLOCK = 256
RET_BLOCKS_PER_STEP = 2
CHUNKS_PER_BLOCK = SEQ_BLOCK // CHUNK
WINDOW_BLOCKS = 1 + PAST_CHUNKS // CHUNKS_PER_BLOCK
WINDOW = WINDOW_BLOCKS * SEQ_BLOCK
LANE = 128
SUBLANES = 8
BIAS_VEC = 1024
REL_PAD = 640
FFN_CHUNK = 256
VMEM_LIMIT_BYTES = 56 * 1024 * 1024


def _resident(shape):
    zeros = (0,) * len(shape)
    return pl.BlockSpec(shape, lambda *_: zeros, pipeline_mode=pl.Buffered(1))


def _params(*semantics):
    return pltpu.CompilerParams(dimension_semantics=semantics,
                                vmem_limit_bytes=VMEM_LIMIT_BYTES)


def _rms_scale(x):
    return x * lax.rsqrt(jnp.mean(x * x, axis=-1, keepdims=True) + EPS)


def _dot(a, b):
    return jnp.dot(a, b, preferred_element_type=F32)


def _dot_nt(a, b):
    return lax.dot_general(a, b, (((1,), (1,)), ((), ())), preferred_element_type=F32)


def _silu(x):
    return x * jax.nn.sigmoid(x)


def _store_seq_blocks(ref, row_lo, value):
    for blk in range(value.shape[1] // SEQ_BLOCK):
        ref[0, blk, row_lo:row_lo + value.shape[0], :] = value[:, blk * SEQ_BLOCK:(blk + 1) * SEQ_BLOCK]


def _inproj_kernel(x_ref, g_ref, wq_ref, wkT_ref, wv_ref, wg_ref,
                   cos_ref, sin_ref, cosT_ref, sinT_ref,
                   q_ref, kT_ref, v_ref, gate_ref):
    h = (_rms_scale(x_ref[...]) * g_ref[...]).astype(BF16)
    cos, sin = cos_ref[...], sin_ref[...]
    cosT, sinT = cosT_ref[...], sinT_ref[...]
    half = RET_QK_DIM // 2
    k_scale = RET_QK_DIM ** -0.5
    for hd in range(RET_HEADS):
        lo = hd * RET_QK_DIM
        qh = _dot(h, wq_ref[:, lo:lo + RET_QK_DIM])
        x1, x2 = qh[:, :half], qh[:, half:]
        q_ref[:, lo:lo + half] = (x1 * cos - x2 * sin).astype(BF16)
        q_ref[:, lo + half:lo + RET_QK_DIM] = (x1 * sin + x2 * cos).astype(BF16)
        kh = _dot_nt(wkT_ref[lo:lo + RET_QK_DIM, :], h)
        y1, y2 = kh[:half], kh[half:]
        _store_seq_blocks(kT_ref, lo, ((y1 * cosT - y2 * sinT) * k_scale).astype(BF16))
        _store_seq_blocks(kT_ref, lo + half, ((y1 * sinT + y2 * cosT) * k_scale).astype(BF16))
    for hd in range(RET_HEADS):
        lo = hd * RET_V_DIM
        v_ref[:, lo:lo + RET_V_DIM] = _dot(h, wv_ref[:, lo:lo + RET_V_DIM]).astype(BF16)
        gate_ref[:, lo:lo + RET_V_DIM] = _silu(_dot(h, wg_ref[:, lo:lo + RET_V_DIM])).astype(BF16)


def _inproj(xf, norm_g, w_in, batch, seq):
    t, d = xf.shape
    q_cols = RET_HEADS * RET_QK_DIM
    v_cols = RET_HEADS * RET_V_DIM
    half = RET_QK_DIM // 2
    wq = w_in[:, :q_cols].astype(BF16)
    wkT = w_in[:, q_cols:2 * q_cols].T.astype(BF16)
    wv = w_in[:, 2 * q_cols:2 * q_cols + v_cols].astype(BF16)
    wg = w_in[:, 2 * q_cols + v_cols:].astype(BF16)
    inv_freq = ROPE_BASE ** (-np.arange(half, dtype=np.float64) / half)
    ang = np.arange(seq, dtype=np.float64)[:, None] * inv_freq[None, :]
    cos, sin = np.cos(ang).astype(np.float32), np.sin(ang).astype(np.float32)
    tiles_per_seq = seq // ROW_TILE
    row_spec = lambda cols: pl.BlockSpec((ROW_TILE, cols), lambda i: (i, 0))
    return pl.pallas_call(
        _inproj_kernel,
        grid=(t // ROW_TILE,),
        in_specs=[
            row_spec(d),
            _resident((1, d)),
            _resident((d, q_cols)),
            _resident((q_cols, d)),
            _resident((d, v_cols)),
            _resident((d, v_cols)),
            pl.BlockSpec((ROW_TILE, half), lambda i: (i % tiles_per_seq, 0)),
            pl.BlockSpec((ROW_TILE, half), lambda i: (i % tiles_per_seq, 0)),
            pl.BlockSpec((half, ROW_TILE), lambda i: (0, i % tiles_per_seq)),
            pl.BlockSpec((half, ROW_TILE), lambda i: (0, i % tiles_per_seq)),
        ],
        out_specs=[
            row_spec(q_cols),
            pl.BlockSpec((1, ROW_TILE // SEQ_BLOCK, q_cols, SEQ_BLOCK),
                         lambda i: (i // tiles_per_seq, i % tiles_per_seq, 0, 0)),
            row_spec(v_cols),
            row_spec(v_cols),
        ],
        out_shape=[
            jax.ShapeDtypeStruct((t, q_cols), BF16),
            jax.ShapeDtypeStruct((batch, seq // SEQ_BLOCK, q_cols, SEQ_BLOCK), BF16),
            jax.ShapeDtypeStruct((t, v_cols), BF16),
            jax.ShapeDtypeStruct((t, v_cols), BF16),
        ],
        compiler_params=_params("parallel"),
        name="inproj",
    )(xf, norm_g.reshape(1, d), wq, wkT, wv, wg, cos, sin,
      np.ascontiguousarray(cos.T), np.ascontiguousarray(sin.T))


def _retention_kernel(q_ref, kT_ref, v_ref, gate_ref, x_ref,
                      qdec_ref, kdec_ref, intra_ref, sdec_ref, gn_ref, wo_ref,
                      o_ref, state_ref, y_ref):
    @pl.when(pl.program_id(1) == 0)
    def _():
        state_ref[...] = jnp.zeros_like(state_ref)

    for sub in range(RET_BLOCKS_PER_STEP):
        rows = slice(sub * SEQ_BLOCK, (sub + 1) * SEQ_BLOCK)
        for hd in range(RET_HEADS):
            ql, vl = hd * RET_QK_DIM, hd * RET_V_DIM
            qh = q_ref[rows, ql:ql + RET_QK_DIM]
            kTh = kT_ref[0, sub, ql:ql + RET_QK_DIM, :]
            vh = v_ref[rows, vl:vl + RET_V_DIM]
            state = state_ref[hd]
            scores = _dot(qh, kTh)
            qd = (qh.astype(F32) * qdec_ref[:, ql:ql + RET_QK_DIM]).astype(BF16)
            cross = _dot(qd, state.astype(BF16))
            kd = (kTh.astype(F32) * kdec_ref[ql:ql + RET_QK_DIM, :]).astype(BF16)
            state_ref[hd] = state * sdec_ref[hd] + _dot(kd, vh)
            o = _dot((scores * intra_ref[hd]).astype(BF16), vh) + cross
            on = _rms_scale(o) * gn_ref[:, vl:vl + RET_V_DIM]
            y_ref[rows, vl:vl + RET_V_DIM] = (
                gate_ref[rows, vl:vl + RET_V_DIM].astype(F32) * on).astype(BF16)
    o_ref[...] = x_ref[...] + _dot(y_ref[...], wo_ref[...])


def _retention_tables():
    lg = np.log(1.0 - 2.0 ** (-5.0 - np.arange(RET_HEADS, dtype=np.float64)))
    pos = np.arange(SEQ_BLOCK, dtype=np.float64)
    chunk_of = np.arange(SEQ_BLOCK) // CHUNK
    visible = chunk_of[None, :] <= chunk_of[:, None]
    intra = np.exp(lg[:, None, None] * np.abs(pos[:, None] - pos[None, :]))
    intra = np.where(visible[None], intra, 0.0)
    q_dec = np.exp(lg[:, None] * (pos[None, :] + 1.0))
    k_dec = np.exp(lg[:, None] * (SEQ_BLOCK - 1.0 - pos[None, :]))
    s_dec = np.exp(lg * SEQ_BLOCK)
    qdec_full = np.repeat(q_dec.T, RET_QK_DIM, axis=1)
    kdec_full = np.repeat(k_dec, RET_QK_DIM, axis=0)
    sdec_full = np.broadcast_to(s_dec[:, None, None], (RET_HEADS, 1, RET_V_DIM))
    return tuple(np.ascontiguousarray(a, dtype=np.float32)
                 for a in (qdec_full, kdec_full, intra, sdec_full))


def _retention(q, kT, v, gate, xf, gn_g, w_o, batch, seq):
    t, d = xf.shape
    q_cols = RET_HEADS * RET_QK_DIM
    v_cols = RET_HEADS * RET_V_DIM
    step_rows = RET_BLOCKS_PER_STEP * SEQ_BLOCK
    nstep = seq // step_rows
    qdec, kdec, intra, sdec = _retention_tables()
    row_spec = lambda cols: pl.BlockSpec((step_rows, cols), lambda b, j: (b * nstep + j, 0))
    return pl.pallas_call(
        _retention_kernel,
        grid=(batch, nstep),
        in_specs=[
            row_spec(q_cols),
            pl.BlockSpec((1, RET_BLOCKS_PER_STEP, q_cols, SEQ_BLOCK), lambda b, j: (b, j, 0, 0)),
            row_spec(v_cols),
            row_spec(v_cols),
            row_spec(d),
            _resident((SEQ_BLOCK, q_cols)),
            _resident((q_cols, SEQ_BLOCK)),
            _resident((RET_HEADS, SEQ_BLOCK, SEQ_BLOCK)),
            _resident((RET_HEADS, 1, RET_V_DIM)),
            _resident((1, v_cols)),
            _resident((v_cols, d)),
        ],
        out_specs=row_spec(d),
        out_shape=jax.ShapeDtypeStruct((t, d), F32),
        scratch_shapes=[
            pltpu.VMEM((RET_HEADS, RET_QK_DIM, RET_V_DIM), F32),
            pltpu.VMEM((step_rows, v_cols), BF16),
        ],
        compiler_params=_params("parallel", "arbitrary"),
        name="retention",
    )(q, kT, v, gate, xf, qdec, kdec, intra, sdec, gn_g.reshape(1, v_cols), w_o.astype(BF16))


def _ffn_kernel(x_ref, g_ref, wgu_ref, wd_ref, o_ref, act_ref):
    x = x_ref[...]
    h = (_rms_scale(x) * g_ref[...]).astype(BF16)
    hidden = wd_ref.shape[0]
    for c in range(0, hidden, FFN_CHUNK):
        gate = _dot(h, wgu_ref[:, c:c + FFN_CHUNK])
        up = _dot(h, wgu_ref[:, hidden + c:hidden + c + FFN_CHUNK])
        act_ref[:, c:c + FFN_CHUNK] = (_silu(gate) * up).astype(BF16)
    o_ref[...] = x + _dot(act_ref[...], wd_ref[...])


def _ffn(xf, norm_g, w_gu, w_down):
    t, d = xf.shape
    hidden = w_down.shape[0]
    row_spec = pl.BlockSpec((FFN_ROW_TILE, d), lambda i: (i, 0))
    return pl.pallas_call(
        _ffn_kernel,
        grid=(t // FFN_ROW_TILE,),
        in_specs=[row_spec, _resident((1, d)), _resident((d, 2 * hidden)), _resident((hidden, d))],
        out_specs=row_spec,
        out_shape=jax.ShapeDtypeStruct((t, d), F32),
        scratch_shapes=[pltpu.VMEM((FFN_ROW_TILE, hidden), BF16)],
        compiler_params=_params("parallel"),
        name="ffn",
    )(xf, norm_g.reshape(1, d), w_gu.astype(BF16), w_down.astype(BF16))


def _kvq_kernel(x_ref, gkv_ref, gq_ref, wkT_ref, wvT_ref, wqT_ref, kg_ref, qg_ref,
                k_ref, vT_ref, qT_ref):
    xn = _rms_scale(x_ref[...])
    u = (xn * gkv_ref[...]).astype(BF16)
    hq = (xn * gq_ref[...]).astype(BF16)
    rows = x_ref.shape[0]
    d = x_ref.shape[1]
    dh = ATT_HEAD_DIM
    kg, qg = kg_ref[...], qg_ref[...]
    ones = jnp.ones((dh, rows), BF16)
    zeros = jnp.zeros((dh, rows), BF16)
    group = 256
    for c in range(0, d, group):
        kT = _dot_nt(wkT_ref[c:c + group, :], u)
        kn = []
        for j in range(0, group, dh):
            kh = kT[j:j + dh]
            kn.append(kh * lax.rsqrt(jnp.mean(kh * kh, axis=0, keepdims=True) + EPS) * kg)
        k_ref[:, c:c + group] = jnp.concatenate(kn, axis=0).T.astype(BF16)
        vT = _dot_nt(wvT_ref[c:c + group, :], u)
        qT = _dot_nt(wqT_ref[c:c + group, :], hq)
        for j in range(0, group, dh):
            head = (c + j) // dh
            lo = 2 * dh * head
            _store_seq_blocks(vT_ref, lo, vT[j:j + dh].astype(BF16))
            _store_seq_blocks(vT_ref, lo + dh, ones)
            qh = qT[j:j + dh]
            qn = (qh * lax.rsqrt(jnp.mean(qh * qh, axis=0, keepdims=True) + EPS) * qg).astype(BF16)
            _store_seq_blocks(qT_ref, lo + dh * (head % 2), qn)
            _store_seq_blocks(qT_ref, lo + dh * (1 - head % 2), zeros)


def _kvq(xf, kv_norm_g, w_kv, k_norm_g, q_in_norm_g, w_q, q_norm_g, batch, seq):
    t, d = xf.shape
    tiles_per_seq = seq // KVQ_ROW_TILE
    row_spec = pl.BlockSpec((KVQ_ROW_TILE, d), lambda i: (i, 0))
    blocked = pl.BlockSpec((1, KVQ_ROW_TILE // SEQ_BLOCK, 2 * d, SEQ_BLOCK),
                           lambda i: (i // tiles_per_seq, i % tiles_per_seq, 0, 0))
    blocked_shape = jax.ShapeDtypeStruct((batch, seq // SEQ_BLOCK, 2 * d, SEQ_BLOCK), BF16)
    return pl.pallas_call(
        _kvq_kernel,
        grid=(t // KVQ_ROW_TILE,),
        in_specs=[
            row_spec, _resident((1, d)), _resident((1, d)),
            _resident((d, d)), _resident((d, d)), _resident((d, d)),
            _resident((ATT_HEAD_DIM, 1)), _resident((ATT_HEAD_DIM, 1)),
        ],
        out_specs=[row_spec, blocked, blocked],
        out_shape=[jax.ShapeDtypeStruct((t, d), BF16), blocked_shape, blocked_shape],
        compiler_params=_params("parallel"),
        name="kvq",
    )(xf, kv_norm_g.reshape(1, d), q_in_norm_g.reshape(1, d),
      w_kv[:, :d].T.astype(BF16), w_kv[:, d:].T.astype(BF16), w_q.T.astype(BF16),
      k_norm_g.reshape(ATT_HEAD_DIM, 1),
      (q_norm_g * (ATT_HEAD_DIM ** -0.5 * LOG2_E)).reshape(ATT_HEAD_DIM, 1))


def _bias_table_kernel(rb_ref, tab_ref):
    lane = lax.broadcasted_iota(jnp.int32, (REL_PAD, BIAS_VEC), 1)
    entry = lax.broadcasted_iota(jnp.int32, (REL_PAD, BIAS_VEC), 0)
    query_minus_key = jnp.where(lane < SEQ_BLOCK, lane, lane - BIAS_VEC)
    dist = (WINDOW - SEQ_BLOCK) + query_minus_key
    idx = jnp.clip(dist, -REL_CLIP, REL_CLIP) + REL_CLIP
    onehot = (entry == idx).astype(BF16)
    rb = rb_ref[...]
    hi = rb.astype(BF16)
    rest = rb - hi.astype(F32)
    mid = rest.astype(BF16)
    low = (rest - mid.astype(F32)).astype(BF16)
    vec = (_dot(hi, onehot) + _dot(mid, onehot) + _dot(low, onehot)) * LOG2_E
    k_chunk = lax.broadcasted_iota(jnp.int32, (WINDOW, SEQ_BLOCK), 0) // CHUNK
    q_chunk = lax.broadcasted_iota(jnp.int32, (WINDOW, SEQ_BLOCK), 1) // CHUNK
    visible = (k_chunk >= q_chunk) & (k_chunk <= q_chunk + PAST_CHUNKS)
    for h in range(ATT_HEADS):
        rows = jnp.broadcast_to(vec[h:h + 1, :], (WINDOW, BIAS_VEC))
        toeplitz = pltpu.roll(rows, 0, 1, stride=1, stride_axis=0)
        tab_ref[h] = jnp.where(visible, toeplitz[:, :SEQ_BLOCK], -jnp.inf)


def _bias_table(rel_bias):
    heads, table = rel_bias.shape
    rb = jnp.pad(rel_bias, ((0, 0), (0, REL_PAD - table)))
    return pl.pallas_call(
        _bias_table_kernel,
        out_shape=jax.ShapeDtypeStruct((heads, WINDOW, SEQ_BLOCK), F32),
        compiler_params=pltpu.CompilerParams(vmem_limit_bytes=VMEM_LIMIT_BYTES),
        name="bias_table",
    )(rb)


SOFTMAX_LAG = 1
VALUE_LAG = 2
ATT_BLOCKS_PER_STEP = 2


def _attention_kernel(k_ref, vT_ref, qT_ref, x_ref, tab_ref, wo_ref, o_ref,
                      kprev_ref, vprev_ref, ctxT_ref):
    dh = ATT_HEAD_DIM
    older = WINDOW_BLOCKS - 1

    def key_piece(sub, p, lanes):
        n = sub + p - older
        if n < 0:
            return kprev_ref[ATT_BLOCKS_PER_STEP + n, :, lanes]
        return k_ref[n * SEQ_BLOCK:(n + 1) * SEQ_BLOCK, lanes]

    def value_piece(sub, p, rows):
        n = sub + p - older
        if n < 0:
            return vprev_ref[ATT_BLOCKS_PER_STEP + n, rows, :]
        return vT_ref[0, n, rows, :]

    units = [(sub, h) for sub in range(ATT_BLOCKS_PER_STEP) for h in range(ATT_HEADS)]
    in_flight = {}

    def scores(u, first_step):
        sub, h = units[u]
        pair = slice((h // 2) * 2 * dh, (h // 2 + 1) * 2 * dh)
        q_pair = qT_ref[0, sub, 2 * dh * h:2 * dh * (h + 1), :]
        running_max = None
        pieces = {}
        for p in range(WINDOW_BLOCKS):
            if first_step and sub + p - older < 0:
                continue
            rows = slice(p * SEQ_BLOCK, (p + 1) * SEQ_BLOCK)
            sp = _dot(key_piece(sub, p, pair), q_pair) + tab_ref[h, rows, :]
            pieces[p] = sp
            for r in range(0, SEQ_BLOCK, SUBLANES):
                tile = sp[r:r + SUBLANES]
                running_max = tile if running_max is None else jnp.maximum(running_max, tile)
        in_flight[u] = (pieces, running_max)

    def softmax(u):
        pieces, running_max = in_flight[u]
        m = jnp.max(running_max, axis=0, keepdims=True)
        in_flight[u] = {p: jnp.exp2(sp - m).astype(BF16) for p, sp in pieces.items()}

    def values(u):
        sub, h = units[u]
        acc = None
        for p, prob in in_flight.pop(u).items():
            part = _dot(value_piece(sub, p, slice(2 * dh * h, 2 * dh * (h + 1))), prob)
            acc = part if acc is None else acc + part
        ctxT_ref[dh * h:dh * (h + 1), sub * SEQ_BLOCK:(sub + 1) * SEQ_BLOCK] = (
            acc[:dh] / acc[dh:])

    def all_units(first_step):
        in_flight.clear()
        for step in range(len(units) + VALUE_LAG):
            if step < len(units):
                scores(step, first_step)
            if 0 <= step - SOFTMAX_LAG < len(units):
                softmax(step - SOFTMAX_LAG)
            if step >= VALUE_LAG:
                values(step - VALUE_LAG)

    @pl.when(pl.program_id(1) > 0)
    def _():
        all_units(False)

    @pl.when(pl.program_id(1) == 0)
    def _():
        all_units(True)

    o_ref[...] = x_ref[...] + _dot(ctxT_ref[...].T.astype(BF16), wo_ref[...])
    for n in range(ATT_BLOCKS_PER_STEP):
        kprev_ref[n] = k_ref[n * SEQ_BLOCK:(n + 1) * SEQ_BLOCK, :]
        vprev_ref[n] = vT_ref[0, n]


def _attention(k, vT, qT, xf, table, w_o, batch, seq):
    t, d = xf.shape
    step_rows = ATT_BLOCKS_PER_STEP * SEQ_BLOCK
    nstep = seq // step_rows
    row_spec = pl.BlockSpec((step_rows, d), lambda b, j: (b * nstep + j, 0))
    blocked = pl.BlockSpec((1, ATT_BLOCKS_PER_STEP, 2 * d, SEQ_BLOCK), lambda b, j: (b, j, 0, 0))
    return pl.pallas_call(
        _attention_kernel,
        grid=(batch, nstep),
        in_specs=[row_spec, blocked, blocked, row_spec,
                  _resident((ATT_HEADS, WINDOW, SEQ_BLOCK)), _resident((d, d))],
        out_specs=row_spec,
        out_shape=jax.ShapeDtypeStruct((t, d), F32),
        scratch_shapes=[pltpu.VMEM((ATT_BLOCKS_PER_STEP, SEQ_BLOCK, d), BF16),
                        pltpu.VMEM((ATT_BLOCKS_PER_STEP, 2 * d, SEQ_BLOCK), BF16),
                        pltpu.VMEM((d, step_rows), F32)],
        compiler_params=_params("arbitrary", "arbitrary"),
        name="attention",
    )(k, vT, qT, xf, table, w_o.astype(BF16))


def kernel(x, a_norm_g, a_w_in, a_gn_g, a_w_o, a_ffn_norm_g, a_w_gu, a_w_down,
           kv_norm_g, w_kv, k_norm_g,
           b_norm_g, b_w_q, b_q_norm_g, b_rel_bias, b_w_o, b_ffn_norm_g, b_w_gu, b_w_down):
    batch, seq, d = x.shape
    assert seq % ROW_TILE == 0 and seq % KVQ_ROW_TILE == 0 and seq % SEQ_BLOCK == 0
    assert (batch * seq) % FFN_ROW_TILE == 0 and seq % (RET_BLOCKS_PER_STEP * SEQ_BLOCK) == 0
    assert WINDOW_BLOCKS * CHUNKS_PER_BLOCK >= PAST_CHUNKS + CHUNKS_PER_BLOCK
    assert ATT_BLOCKS_PER_STEP >= WINDOW_BLOCKS - 1 and seq % (ATT_BLOCKS_PER_STEP * SEQ_BLOCK) == 0
    assert b_norm_g.shape[0] == 1, "the query projection is fused with the shared K/V projection"
    xf = x.reshape(batch * seq, d)
    for i in range(a_norm_g.shape[0]):
        q, kT, v, gate = _inproj(xf, a_norm_g[i], a_w_in[i], batch, seq)
        xf = _retention(q, kT, v, gate, xf, a_gn_g[i], a_w_o[i], batch, seq)
        xf = _ffn(xf, a_ffn_norm_g[i], a_w_gu[i], a_w_down[i])
    k, vT, qT = _kvq(xf, kv_norm_g, w_kv, k_norm_g, b_norm_g[0], b_w_q[0], b_q_norm_g[0], batch, seq)
    table = _bias_table(b_rel_bias[0])
    xf = _attention(k, vT, qT, xf, table, b_w_o[0], batch, seq)
    xf = _ffn(xf, b_ffn_norm_g[0], b_w_gu[0], b_w_down[0])
    return xf.reshape(batch, seq, d)
```
